```python
import jax, jax.numpy as jnp
from jax import lax
import numpy as np

D_MODEL = 2048
BATCH = 4
SEQ = 2048
DEPTH = 4
DEC_BATCH = 128
DEC_SEQ = 4
PAST_LEN = 16384
PAGE_SIZE = 128

PLE_DIM = 256
RMS_EPS = 1e-6
SSD_EXPAND = 2
SSD_INNER = SSD_EXPAND * D_MODEL
SSD_HEADDIM = 64
SSD_HEADS = SSD_INNER // SSD_HEADDIM
SSD_GROUPS = 8
SSD_HPG = SSD_HEADS // SSD_GROUPS
SSD_STATE = 128
SSD_CHUNK = 128
CONV_W = 4
CONV_DIM = SSD_INNER + 2 * SSD_GROUPS * SSD_STATE
CMLP_WIDTH = D_MODEL
CMLP_GROUPS = 16
CMLP_GROUP_DIM = CMLP_WIDTH // CMLP_GROUPS
CMLP_CHUNK = 128
FFN_HIDDEN = -(-8 * D_MODEL // (3 * 256)) * 256
OFF_Z = SSD_INNER
OFF_XBC = OFF_Z + CONV_DIM
OFF_DT = OFF_XBC + SSD_HEADS
OFF_U = OFF_DT + CMLP_WIDTH
OFF_V = OFF_U + CMLP_WIDTH
OFF_GA = OFF_V + D_MODEL
IN_WIDTH = OFF_GA + D_MODEL

kernel_name = "hybrid_ssd_chunkmlp_decoder_step"


def rmsnorm(x, g):
    xf = x.astype(jnp.float32)
    r = lax.rsqrt(jnp.mean(xf * xf, axis=-1, keepdims=True) + RMS_EPS)
    return (xf * r).astype(x.dtype) * g


def causal_conv(xbc, conv0, conv_w, conv_b):
    L = xbc.shape[1]
    xpad = jnp.concatenate([conv0.astype(xbc.dtype), xbc], axis=1)
    out = conv_b + sum(xpad[:, k:k + L] * conv_w[k] for k in range(CONV_W))
    return out, xpad[:, L:]


def ssd_scan(xs, dt, a, bm, cm, h0):
    f32 = jnp.float32
    nb, L = xs.shape[0], xs.shape[1]
    q = min(SSD_CHUNK, L)
    nc = L // q
    x6 = xs.reshape(nb, nc, q, SSD_GROUPS, SSD_HPG, SSD_HEADDIM).astype(f32)
    dt5 = dt.reshape(nb, nc, q, SSD_GROUPS, SSD_HPG)
    xdt = x6 * dt5[..., None]
    cum = jnp.cumsum(dt5 * a.reshape(SSD_GROUPS, SSD_HPG).astype(f32), axis=2)
    b5 = bm.reshape(nb, nc, q, SSD_GROUPS, SSD_STATE).astype(f32)
    c5 = cm.reshape(nb, nc, q, SSD_GROUPS, SSD_STATE).astype(f32)
    causal = jnp.tril(jnp.ones((q, q), dtype=bool))[None, None, :, :, None, None]
    seg = jnp.exp(jnp.where(causal, cum[:, :, :, None] - cum[:, :, None, :], -jnp.inf))
    cb = jnp.einsum('bctgn,bcsgn->bctsg', c5, b5)
    y_diag = jnp.einsum('bctsgh,bcsghp->bctghp', cb[..., None] * seg, xdt)
    decay_to_end = jnp.exp(cum[:, :, -1:] - cum)
    chunk_states = jnp.einsum('bcsgn,bcsghp->bcghpn', b5, xdt * decay_to_end[..., None])
    chunk_decay = jnp.exp(cum[:, :, -1])

    def step(h, inp):
        dec, st = inp
        return dec[..., None, None] * h + st, h

    h_init = h0.astype(f32).reshape(nb, SSD_GROUPS, SSD_HPG, SSD_HEADDIM, SSD_STATE)
    h_final, h_starts = lax.scan(step, h_init, (jnp.moveaxis(chunk_decay, 1, 0), jnp.moveaxis(chunk_states, 1, 0)))
    h_starts = jnp.moveaxis(h_starts, 0, 1)
    y_off = jnp.einsum('bctgn,bcghpn->bctghp', c5, h_starts) * jnp.exp(cum)[..., None]
    y = (y_diag + y_off).reshape(nb, L, SSD_HEADS, SSD_HEADDIM).astype(xs.dtype)
    return y, h_final.reshape(nb, SSD_HEADS, SSD_HEADDIM, SSD_STATE)


def chunk_mlp(u, v, w_s, b_s):
    nb, L = v.shape[0], v.shape[1]
    q = min(CMLP_CHUNK, L)
    nc = L // q
    v5 = v.reshape(nb, nc, q, CMLP_GROUPS, CMLP_GROUP_DIM)
    wm = w_s[:, :q, :q] * jnp.tril(jnp.ones((q, q), dtype=w_s.dtype))
    mixed = jnp.einsum('gts,bcsgd->bctgd', wm, v5) + b_s[:, :q].T[:, :, None]
    return u * mixed.reshape(nb, L, CMLP_WIDTH)


def trunk_layer(x, ple, h0, conv0, g_mix, w_in, conv_w, conv_b, dt_bias, a_log, d_skip, g_ssd, w_br_a,
                g_v, w_s, b_s, w_br_b, w_out, g_ffn, w_gate_up, w_down, g_ple, w_ple_gate, w_ple):
    nb, L = x.shape[0], x.shape[1]
    h = rmsnorm(x, g_mix)
    proj = h @ w_in
    z, xbc, dt_raw, u, v, ga, gb = jnp.split(proj, [OFF_Z, OFF_XBC, OFF_DT, OFF_U, OFF_V, OFF_GA], axis=-1)
    xbc_c, conv_new = causal_conv(xbc, conv0, conv_w, conv_b)
    xbc_c = jax.nn.silu(xbc_c)
    xs = xbc_c[..., :SSD_INNER].reshape(nb, L, SSD_HEADS, SSD_HEADDIM)
    bm = xbc_c[..., SSD_INNER:SSD_INNER + SSD_GROUPS * SSD_STATE].reshape(nb, L, SSD_GROUPS, SSD_STATE)
    cm = xbc_c[..., SSD_INNER + SSD_GROUPS * SSD_STATE:].reshape(nb, L, SSD_GROUPS, SSD_STATE)
    dt = jax.nn.softplus((dt_raw + dt_bias).astype(jnp.float32))
    a = -jnp.exp(a_log.astype(jnp.float32))
    y, h_new = ssd_scan(xs, dt, a, bm, cm, h0)
    y = (y + xs * d_skip[:, None]).reshape(nb, L, SSD_INNER)
    y = rmsnorm(y * jax.nn.silu(z), g_ssd)
    ya = y @ w_br_a
    u = jax.nn.gelu(u)
    v = rmsnorm(jax.nn.gelu(v), g_v)
    yb = chunk_mlp(u, v, w_s, b_s) @ w_br_b
    q = min(CMLP_CHUNK, L)
    v_state = v[:, L - q:]
    x = x + (jax.nn.sigmoid(ga) * ya + jax.nn.sigmoid(gb) * yb) @ w_out
    gu = rmsnorm(x, g_ffn) @ w_gate_up
    g_act, up = jnp.split(gu, [FFN_HIDDEN], axis=-1)
    x = x + (jax.nn.silu(g_act) * up) @ w_down
    x = x + jax.nn.sigmoid(rmsnorm(x, g_ple) @ w_ple_gate) * (ple @ w_ple)
    return x, h_new, conv_new, v_state


def setup_inputs(seed: int = 0) -> dict:
    key = jax.random.key(seed)
    ks = jax.random.split(key, 32)
    f32 = jnp.float32

    def nrm(k, shape, scale):
        return jax.random.normal(k, shape, f32) * scale

    def gain(k, shape):
        return 1.0 + 0.02 * jax.random.normal(k, shape, f32)

    dt0 = jnp.exp(jax.random.uniform(ks[10], (DEPTH, SSD_HEADS), f32, np.log(1e-3), np.log(1e-1)))
    return {
        "x_prompt": nrm(ks[0], (BATCH, SEQ, D_MODEL), 1.0),
        "x_sample": nrm(ks[1], (DEC_BATCH, DEC_SEQ, D_MODEL), 1.0),
        "p_prompt": nrm(ks[2], (DEPTH, BATCH, SEQ, PLE_DIM), 1.0),
        "p_sample": nrm(ks[3], (DEPTH, DEC_BATCH, DEC_SEQ, PLE_DIM), 1.0),
        "state_ssd": nrm(ks[4], (DEPTH, DEC_BATCH, SSD_HEADS, SSD_HEADDIM, SSD_STATE), 0.3),
        "state_conv": nrm(ks[5], (DEPTH, DEC_BATCH, CONV_W - 1, CONV_DIM), 1.0),
        "g_mix": gain(ks[6], (DEPTH, D_MODEL)),
        "w_in": nrm(ks[7], (DEPTH, D_MODEL, IN_WIDTH), D_MODEL ** -0.5),
        "conv_w": nrm(ks[8], (DEPTH, CONV_W, CONV_DIM), CONV_W ** -0.5),
        "conv_b": nrm(ks[9], (DEPTH, CONV_DIM), 0.02),
        "dt_bias": dt0 + jnp.log(-jnp.expm1(-dt0)),
        "a_log": jnp.log(jax.random.uniform(ks[11], (DEPTH, SSD_HEADS), f32, 1.0, 16.0)),
        "d_skip": gain(ks[12], (DEPTH, SSD_HEADS)),
        "g_ssd": gain(ks[13], (DEPTH, SSD_INNER)),
        "w_br_a": nrm(ks[14], (DEPTH, SSD_INNER, D_MODEL), SSD_INNER ** -0.5),
        "g_v": gain(ks[15], (DEPTH, CMLP_WIDTH)),
        "w_s": nrm(ks[16], (DEPTH, CMLP_GROUPS, CMLP_CHUNK, CMLP_CHUNK), CMLP_CHUNK ** -0.5),
        "b_s": gain(ks[17], (DEPTH, CMLP_GROUPS, CMLP_CHUNK)),
        "w_br_b": nrm(ks[18], (DEPTH, CMLP_WIDTH, D_MODEL), CMLP_WIDTH ** -0.5),
        "w_out": nrm(ks[19], (DEPTH, D_MODEL, D_MODEL), D_MODEL ** -0.5),
        "g_ffn": gain(ks[20], (DEPTH, D_MODEL)),
        "w_gate_up": nrm(ks[21], (DEPTH, D_MODEL, 2 * FFN_HIDDEN), D_MODEL ** -0.5),
        "w_down": nrm(ks[22], (DEPTH, FFN_HIDDEN, D_MODEL), FFN_HIDDEN ** -0.5),
        "g_ple": gain(ks[23], (DEPTH, D_MODEL)),
        "w_ple_gate": nrm(ks[24], (DEPTH, D_MODEL, D_MODEL), D_MODEL ** -0.5),
        "w_ple": nrm(ks[25], (DEPTH, PLE_DIM, D_MODEL), PLE_DIM ** -0.5),
        "g_final": gain(ks[26], (D_MODEL,)),
    }


def reference(x_prompt, x_sample, p_prompt, p_sample, state_ssd, state_conv, g_mix, w_in, conv_w, conv_b,
              dt_bias, a_log, d_skip, g_ssd, w_br_a, g_v, w_s, b_s, w_br_b, w_out, g_ffn, w_gate_up, w_down,
              g_ple, w_ple_gate, w_ple, g_final):
    xp, xs = x_prompt, x_sample
    h0_p = jnp.zeros((xp.shape[0], SSD_HEADS, SSD_HEADDIM, SSD_STATE), jnp.float32)
    conv0_p = jnp.zeros((xp.shape[0], CONV_W - 1, CONV_DIM), xp.dtype)
    ssd_p, conv_p, v_p, ssd_s, conv_s, v_s = [], [], [], [], [], []
    for i in range(DEPTH):
        lw = (g_mix[i], w_in[i], conv_w[i], conv_b[i], dt_bias[i], a_log[i], d_skip[i], g_ssd[i], w_br_a[i],
              g_v[i], w_s[i], b_s[i], w_br_b[i], w_out[i], g_ffn[i], w_gate_up[i], w_down[i], g_ple[i],
              w_ple_gate[i], w_ple[i])
        xp, hp, cp, vp = trunk_layer(xp, p_prompt[i], h0_p, conv0_p, *lw)
        xs, hs, cs, vs = trunk_layer(xs, p_sample[i], state_ssd[i], state_conv[i], *lw)
        ssd_p.append(hp); conv_p.append(cp); v_p.append(vp)
        ssd_s.append(hs); conv_s.append(cs); v_s.append(vs)
    y_prompt = rmsnorm(xp, g_final)
    y_sample = rmsnorm(xs, g_final)
    return (y_prompt, y_sample, jnp.stack(ssd_p).astype(xp.dtype), jnp.stack(conv_p), jnp.stack(v_p),
            jnp.stack(ssd_s).astype(xs.dtype), jnp.stack(conv_s), jnp.stack(v_s))
```

```python
import functools

import jax
import jax.numpy as jnp
from jax import lax
from jax.experimental import pallas as pl
from jax.experimental.pallas import tpu as pltpu

F32 = jnp.float32
BF16 = jnp.bfloat16

D_MODEL = 2048
DEPTH = 4
PLE_DIM = 256
RMS_EPS = 1e-6
SSD_INNER = 4096
SSD_HEADDIM = 64
SSD_HEADS = 64
SSD_GROUPS = 8
SSD_HPG = 8
SSD_STATE = 128
SSD_CHUNK = 128
CONV_W = 4
CONV_DIM = 6144
CMLP_WIDTH = 2048
CMLP_GROUPS = 16
CMLP_GROUP_DIM = 128
CMLP_CHUNK = 128
FFN_HIDDEN = 5632
OFF_Z = SSD_INNER
OFF_XBC = OFF_Z + CONV_DIM
OFF_DT = OFF_XBC + SSD_HEADS
IN_WIDTH = OFF_DT + 4 * D_MODEL
GROUP_COLS = SSD_HPG * SSD_HEADDIM
LANES = 128
VMEM_LIMIT_MB = 58


def _params(n_axes, vmem_mb=VMEM_LIMIT_MB):
    return pltpu.CompilerParams(dimension_semantics=("arbitrary",) * n_axes,
                                vmem_limit_bytes=vmem_mb * 2 ** 20)


def _softplus(x):
    return jnp.maximum(x, 0.0) + jnp.log1p(jnp.exp(-jnp.abs(x)))


def _split3(q):
    hi = q.astype(BF16)
    r1 = q - hi.astype(F32)
    mid = r1.astype(BF16)
    lo = (r1 - mid.astype(F32)).astype(BF16)
    return jnp.concatenate([hi, mid, lo], axis=1)


def _rownorm_body(x_ref, g_ref, o_ref):
    xf = x_ref[...]
    r = lax.rsqrt(jnp.mean(xf * xf, axis=-1, keepdims=True) + RMS_EPS)
    o_ref[...] = ((xf * r) * g_ref[...]).astype(o_ref.dtype)


def _rownorm(x, g, out_dtype, tr=512):
    m, d = x.shape
    return pl.pallas_call(
        _rownorm_body,
        grid=(m // tr,),
        in_specs=[pl.BlockSpec((tr, d), lambda i: (i, 0)),
                  pl.BlockSpec((1, d), lambda i: (0, 0))],
        out_specs=pl.BlockSpec((tr, d), lambda i: (i, 0)),
        out_shape=jax.ShapeDtypeStruct((m, d), out_dtype),
        compiler_params=_params(1, 32),
        name="rownorm",
    )(x, g.reshape(1, d))


def _fused_mm_body(*refs, n_acts, dots, n_extras, epi):
    a_refs = refs[:n_acts]
    w_refs = refs[n_acts:n_acts + len(dots)]
    e_refs = refs[n_acts + len(dots):n_acts + len(dots) + n_extras]
    o_ref = refs[n_acts + len(dots) + n_extras]
    scr = refs[n_acts + len(dots) + n_extras + 1:]

    @pl.when(pl.program_id(1) == 0)
    def _():
        for w_ref, s in zip(w_refs, scr):
            s[...] = w_ref[...].astype(BF16)

    acts = [a[...].astype(BF16) for a in a_refs]
    accs = [jnp.dot(acts[ai], s[...], preferred_element_type=F32) for ai, s in zip(dots, scr)]
    o_ref[...] = epi(accs, [e[...] for e in e_refs]).astype(o_ref.dtype)


def _fused_mm(acts, weights, extras, epi, *, n, tm, tn, out_dtype, name):
    m = acts[0].shape[0]
    tm = min(tm, m)
    in_specs, args, scratch = [], [], []
    for a in acts:
        in_specs.append(pl.BlockSpec((tm, a.shape[1]), lambda j, i: (i, 0)))
        args.append(a)
    for ai, w, off in weights:
        k = acts[ai].shape[1]
        in_specs.append(pl.BlockSpec((k, tn), lambda j, i, off=off: (0, j + off)))
        args.append(w)
        scratch.append(pltpu.VMEM((k, tn), BF16))
    for e, off in extras:
        in_specs.append(pl.BlockSpec((tm, tn), lambda j, i, off=off: (i, j + off)))
        args.append(e)
    body = functools.partial(_fused_mm_body, n_acts=len(acts), dots=tuple(w[0] for w in weights),
                             n_extras=len(extras), epi=epi)
    return pl.pallas_call(
        body,
        grid=(n // tn, m // tm),
        in_specs=in_specs,
        out_specs=pl.BlockSpec((tm, tn), lambda j, i: (i, j)),
        out_shape=jax.ShapeDtypeStruct((m, n), out_dtype),
        scratch_shapes=scratch,
        compiler_params=_params(2),
        name=name,
    )(*args)


def _epi_first(accs, extras):
    return accs[0]


def _epi_residual(accs, extras):
    return extras[0] + accs[0]


def _epi_gate_merge(accs, extras):
    return jax.nn.sigmoid(extras[0]) * accs[0] + jax.nn.sigmoid(extras[1]) * accs[1]


def _epi_swiglu(accs, extras):
    return jax.nn.silu(accs[0]) * accs[1]


def _epi_ple(accs, extras):
    return extras[0] + jax.nn.sigmoid(accs[0]) * accs[1]


def _conv_prompt_body(x_ref, w_ref, b_ref, o_ref):
    x = x_ref[...]
    rows = lax.broadcasted_iota(jnp.int32, x.shape, 0)
    acc = b_ref[...] + w_ref[CONV_W - 1:CONV_W, :] * x
    for j in range(1, CONV_W):
        shifted = jnp.where(rows >= j, pltpu.roll(x, j, axis=0), 0.0)
        acc = acc + w_ref[CONV_W - 1 - j:CONV_W - j, :] * shifted
    o_ref[...] = acc * jax.nn.sigmoid(acc)


def _conv_prompt(zx, conv_w, conv_b, nb, seq, tc=512):
    col0 = OFF_Z // tc
    return pl.pallas_call(
        _conv_prompt_body,
        grid=(nb, CONV_DIM // tc),
        in_specs=[pl.BlockSpec((seq, tc), lambda b, c: (b, c + col0)),
                  pl.BlockSpec((CONV_W, tc), lambda b, c: (0, c)),
                  pl.BlockSpec((1, tc), lambda b, c: (0, c))],
        out_specs=pl.BlockSpec((seq, tc), lambda b, c: (b, c)),
        out_shape=jax.ShapeDtypeStruct((nb * seq, CONV_DIM), F32),
        compiler_params=_params(2, 40),
        name="conv_prompt",
    )(zx, conv_w, conv_b.reshape(1, CONV_DIM))


def _conv_sample_body(x_ref, s_ref, w_ref, b_ref, o_ref):
    steps = x_ref.shape[0]
    xpad = [s_ref[k] for k in range(CONV_W - 1)] + [x_ref[t] for t in range(steps)]
    for t in range(steps):
        acc = b_ref[...] + w_ref[0:1, :] * xpad[t]
        for k in range(1, CONV_W):
            acc = acc + w_ref[k:k + 1, :] * xpad[t + k]
        o_ref[t] = acc * jax.nn.sigmoid(acc)


def _conv_sample(xbc_t, state_t, conv_w, conv_b, tc=1024):
    steps, nb, _ = xbc_t.shape
    return pl.pallas_call(
        _conv_sample_body,
        grid=(CONV_DIM // tc,),
        in_specs=[pl.BlockSpec((steps, nb, tc), lambda c: (0, 0, c)),
                  pl.BlockSpec((CONV_W - 1, nb, tc), lambda c: (0, 0, c)),
                  pl.BlockSpec((CONV_W, tc), lambda c: (0, c)),
                  pl.BlockSpec((1, tc), lambda c: (0, c))],
        out_specs=pl.BlockSpec((steps, nb, tc), lambda c: (0, 0, c)),
        out_shape=jax.ShapeDtypeStruct((steps, nb, CONV_DIM), F32),
        compiler_params=_params(1, 32),
        name="conv_sample",
    )(xbc_t, state_t, conv_w, conv_b.reshape(1, CONV_DIM))


def _gated_norm(y2, g):
    r = lax.rsqrt(jnp.mean(y2 * y2, axis=-1, keepdims=True) + RMS_EPS)
    return (y2 * r) * g


def _ssd_prompt_body(xc_ref, z_ref, dtp_ref, bias_ref, alog_ref, dsk_ref, g_ref, e3_ref,
                     yn_ref, st_ref, s_scr, y_scr, *, n_chunks):
    c = pl.program_id(1)
    q = SSD_CHUNK

    @pl.when(c == 0)
    def _():
        s_scr[...] = jnp.zeros_like(s_scr)

    lane = lax.broadcasted_iota(jnp.int32, (q, LANES), 1)
    row = lax.broadcasted_iota(jnp.int32, (q, LANES), 0)
    head_ok = lane < SSD_HEADS
    dt = jnp.where(head_ok, _softplus(dtp_ref[...] + bias_ref[...]), 0.0)
    a = -jnp.exp(alog_ref[...])
    dta = dt * a
    tri = (row >= lane).astype(BF16)
    cum3 = jnp.dot(tri, _split3(dta), preferred_element_type=F32)
    cum = cum3[:, :LANES] + cum3[:, LANES:2 * LANES] + cum3[:, 2 * LANES:]
    cum_t = cum.T
    last = cum[q - 1:q, :]
    ecum = jnp.exp(cum)
    dte = jnp.where(head_ok, jnp.exp(last - cum), 0.0)
    stack = jnp.concatenate([dt, dte, ecum], axis=0)
    wide = jnp.dot(_split3(stack), e3_ref[...], preferred_element_type=F32)
    dt_w, dte_w, ecum_w = wide[:q], wide[q:2 * q], wide[2 * q:]
    cdec_w = ecum_w[q - 1:q, :]

    xs = xc_ref[:, :SSD_INNER]
    xdt = xs * dt_w
    xdt_b = xdt.astype(BF16)
    xdd_b = (xdt * dte_w).astype(BF16)
    causal = row >= lane
    for g in range(SSD_GROUPS):
        gs = slice(g * GROUP_COLS, (g + 1) * GROUP_COLS)
        b_g = xc_ref[:, SSD_INNER + g * SSD_STATE:SSD_INNER + (g + 1) * SSD_STATE]
        c_g = xc_ref[:, SSD_INNER + (SSD_GROUPS + g) * SSD_STATE:SSD_INNER + (SSD_GROUPS + g + 1) * SSD_STATE]
        c_b = c_g.astype(BF16)
        cb = lax.dot_general(c_b, b_g.astype(BF16), (((1,), (1,)), ((), ())), preferred_element_type=F32)
        s_g = s_scr[g]
        y_off = jnp.dot(c_b, s_g.astype(BF16), preferred_element_type=F32)
        y_heads = []
        for hh in range(SSD_HPG):
            h = g * SSD_HPG + hh
            diff = cum[:, h:h + 1] - cum_t[h:h + 1, :]
            seg = jnp.exp(jnp.where(causal, diff, -jnp.inf))
            m_h = (cb * seg).astype(BF16)
            y_heads.append(jnp.dot(m_h, xdt_b[:, h * SSD_HEADDIM:(h + 1) * SSD_HEADDIM],
                                   preferred_element_type=F32))
        y_g = jnp.concatenate(y_heads, axis=1) + y_off * ecum_w[:, gs]
        upd = jnp.dot(b_g.T.astype(BF16), xdd_b[:, gs], preferred_element_type=F32)
        s_new = s_g * cdec_w[:, gs] + upd
        s_scr[g] = s_new

        @pl.when(c == n_chunks - 1)
        def _(s_new=s_new, gs=gs):
            st_ref[0, gs, :] = s_new.T

        zg = z_ref[:, gs]
        y_scr[:, gs] = (y_g + xs[:, gs] * dsk_ref[:, gs]) * (zg * jax.nn.sigmoid(zg))
    yn_ref[...] = _gated_norm(y_scr[...], g_ref[...]).astype(yn_ref.dtype)


def _ssd_prompt(xc, zx, dtp, bias, alog, dsk_w, g_ssd, e3, nb, seq):
    q = SSD_CHUNK
    nc = seq // q
    body = functools.partial(_ssd_prompt_body, n_chunks=nc)
    const = lambda b, c: (0, 0)
    return pl.pallas_call(
        body,
        grid=(nb, nc),
        in_specs=[pl.BlockSpec((q, CONV_DIM), lambda b, c: (b * nc + c, 0)),
                  pl.BlockSpec((q, SSD_INNER), lambda b, c: (b * nc + c, 0)),
                  pl.BlockSpec((q, LANES), lambda b, c: (b * nc + c, 0)),
                  pl.BlockSpec((1, LANES), const),
                  pl.BlockSpec((1, LANES), const),
                  pl.BlockSpec((1, SSD_INNER), const),
                  pl.BlockSpec((1, SSD_INNER), const),
                  pl.BlockSpec((3 * LANES, SSD_INNER), const)],
        out_specs=[pl.BlockSpec((q, SSD_INNER), lambda b, c: (b * nc + c, 0)),
                   pl.BlockSpec((1, SSD_INNER, SSD_STATE), lambda b, c: (b, 0, 0))],
        out_shape=[jax.ShapeDtypeStruct((nb * seq, SSD_INNER), BF16),
                   jax.ShapeDtypeStruct((nb, SSD_INNER, SSD_STATE), F32)],
        scratch_shapes=[pltpu.VMEM((SSD_GROUPS, SSD_STATE, GROUP_COLS), F32),
                        pltpu.VMEM((q, SSD_INNER), F32)],
        compiler_params=_params(2, 48),
        name="ssd_prompt",
    )(xc, zx, dtp, bias, alog, dsk_w, g_ssd, e3)


SEQ_PER_STEP = 4


def _ssd_sample_body(xc_ref, z_ref, dtp_ref, h0_ref, bias_ref, alog_ref, dsk_ref, g_ref, e3_ref,
                     yn_ref, hn_ref, *, steps):
    rows_n = SEQ_PER_STEP * steps
    lane = lax.broadcasted_iota(jnp.int32, (rows_n, LANES), 1)
    row = lax.broadcasted_iota(jnp.int32, (rows_n, LANES), 0)
    tpos = row % steps
    head_ok = lane < SSD_HEADS

    def shift(v, j, tp):
        return jnp.where(tp >= j, pltpu.roll(v, j, axis=0), 0.0)

    dt = jnp.where(head_ok, _softplus(dtp_ref[...] + bias_ref[...]), 0.0)
    a = -jnp.exp(alog_ref[...])
    dta = dt * a
    cum = dta
    for j in range(1, steps):
        cum = cum + shift(dta, j, tpos)
    last = jnp.zeros_like(cum)
    for k in range(steps):
        back = steps - 1 - k
        src = cum if back == 0 else pltpu.roll(cum, rows_n - back, axis=0)
        last = jnp.where(tpos == k, src, last)
    ecum = jnp.exp(cum)
    dte = jnp.where(head_ok, jnp.exp(last - cum), 0.0)
    cdec = jnp.exp(last)

    b_all = xc_ref[:, SSD_INNER:SSD_INNER + SSD_GROUPS * SSD_STATE]
    c_all = xc_ref[:, SSD_INNER + SSD_GROUPS * SSD_STATE:]
    row_w = lax.broadcasted_iota(jnp.int32, b_all.shape, 0)
    tpos_w = row_w % steps
    coefs = []
    for j in range(steps):
        prod = c_all * (b_all if j == 0 else shift(b_all, j, tpos_w))
        cb = jnp.zeros((rows_n, LANES), F32)
        for g in range(SSD_GROUPS):
            cbg = jnp.sum(prod[:, g * SSD_STATE:(g + 1) * SSD_STATE], axis=1, keepdims=True)
            cb = jnp.where((lane // SSD_HPG) == g, cbg, cb)
        if j == 0:
            coefs.append(jnp.where(head_ok, cb, 0.0))
        else:
            seg = jnp.exp(cum - shift(cum, j, tpos))
            coefs.append(jnp.where(head_ok & (tpos >= j), cb * seg, 0.0))
    stack = jnp.concatenate([dt, dte, ecum, cdec] + coefs, axis=0)
    wide = jnp.dot(_split3(stack), e3_ref[...], preferred_element_type=F32)
    dt_w, dte_w = wide[:rows_n], wide[rows_n:2 * rows_n]
    ecum_w, cdec_w = wide[2 * rows_n:3 * rows_n], wide[3 * rows_n:4 * rows_n]

    xs = xc_ref[:, :SSD_INNER]
    xdt = xs * dt_w
    y = wide[4 * rows_n:5 * rows_n] * xdt
    for j in range(1, steps):
        y = y + wide[(4 + j) * rows_n:(5 + j) * rows_n] * pltpu.roll(xdt, j, axis=0)
    xdd = xdt * dte_w

    row_g = lax.broadcasted_iota(jnp.int32, (rows_n, GROUP_COLS), 0)
    ones_b = jnp.ones((rows_n, SSD_STATE), BF16)
    c_bf = c_all.astype(BF16)
    b_bf = b_all.astype(BF16)
    y_off_groups = []
    for g in range(SSD_GROUPS):
        gs = slice(g * GROUP_COLS, (g + 1) * GROUP_COLS)
        ns = slice(g * SSD_STATE, (g + 1) * SSD_STATE)
        y_off = jnp.zeros((rows_n, GROUP_COLS), F32)
        for i in range(SEQ_PER_STEP):
            mine = (row_g // steps) == i
            h0g = h0_ref[i, gs, :]
            yo = lax.dot_general(c_bf[:, ns], h0g.astype(BF16), (((1,), (1,)), ((), ())),
                                 preferred_element_type=F32)
            y_off = jnp.where(mine, yo, y_off)
            x_i = jnp.where(mine, xdd[:, gs], 0.0).astype(BF16)
            upd = lax.dot_general(x_i, b_bf[:, ns], (((0,), (0,)), ((), ())),
                                  preferred_element_type=F32)
            cd = cdec_w[i * steps:i * steps + 1, gs]
            cd_hi = cd.astype(BF16).astype(F32)
            cd_mid = (cd - cd_hi).astype(BF16).astype(F32)
            cd_lo = cd - cd_hi - cd_mid
            cd_rows = jnp.where(row_g == 0, cd_hi,
                                jnp.where(row_g == 1, cd_mid,
                                          jnp.where(row_g == 2, cd_lo, 0.0))).astype(BF16)
            cd_col = lax.dot_general(cd_rows, ones_b, (((0,), (0,)), ((), ())),
                                     preferred_element_type=F32)
            hn_ref[i, gs, :] = h0g * cd_col + upd
        y_off_groups.append(y_off)
    y = y + jnp.concatenate(y_off_groups, axis=1) * ecum_w
    zz = z_ref[...]
    y2 = (y + xs * dsk_ref[...]) * (zz * jax.nn.sigmoid(zz))
    yn_ref[...] = _gated_norm(y2, g_ref[...]).astype(yn_ref.dtype)


def _ssd_sample(xc, z, dtp, h0, bias, alog, dsk_w, g_ssd, e3, steps):
    nb = h0.shape[0]
    rows_n = SEQ_PER_STEP * steps
    body = functools.partial(_ssd_sample_body, steps=steps)
    const = lambda i: (0, 0)
    return pl.pallas_call(
        body,
        grid=(nb // SEQ_PER_STEP,),
        in_specs=[pl.BlockSpec((rows_n, CONV_DIM), lambda i: (i, 0)),
                  pl.BlockSpec((rows_n, SSD_INNER), lambda i: (i, 0)),
                  pl.BlockSpec((rows_n, LANES), lambda i: (i, 0)),
                  pl.BlockSpec((SEQ_PER_STEP, SSD_INNER, SSD_STATE), lambda i: (i, 0, 0)),
                  pl.BlockSpec((1, LANES), const),
                  pl.BlockSpec((1, LANES), const),
                  pl.BlockSpec((1, SSD_INNER), const),
                  pl.BlockSpec((1, SSD_INNER), const),
                  pl.BlockSpec((3 * LANES, SSD_INNER), const)],
        out_specs=[pl.BlockSpec((rows_n, SSD_INNER), lambda i: (i, 0)),
                   pl.BlockSpec((SEQ_PER_STEP, SSD_INNER, SSD_STATE), lambda i: (i, 0, 0))],
        out_shape=[jax.ShapeDtypeStruct((nb * steps, SSD_INNER), BF16),
                   jax.ShapeDtypeStruct((nb, SSD_INNER, SSD_STATE), F32)],
        compiler_params=_params(1, 56),
        name="ssd_sample",
    )(xc, z, dtp, h0, bias, alog, dsk_w, g_ssd, e3)


def _vnorm(v, g):
    v = jax.nn.gelu(v)
    r = lax.rsqrt(jnp.mean(v * v, axis=-1, keepdims=True) + RMS_EPS)
    return (v * r) * g


def _cmlp_prompt_body(u_ref, v_ref, ws_ref, bst_ref, gv_ref, cm_ref, vs_ref, *, n_chunks):
    q = CMLP_CHUNK
    vn = _vnorm(v_ref[...], gv_ref[...])

    @pl.when(pl.program_id(0) % n_chunks == n_chunks - 1)
    def _():
        vs_ref[0] = vn

    u = jax.nn.gelu(u_ref[...])
    row = lax.broadcasted_iota(jnp.int32, (q, q), 0)
    col = lax.broadcasted_iota(jnp.int32, (q, q), 1)
    vb = vn.astype(BF16)
    for g in range(CMLP_GROUPS):
        gs = slice(g * CMLP_GROUP_DIM, (g + 1) * CMLP_GROUP_DIM)
        wm = jnp.where(row >= col, ws_ref[g], 0.0).astype(BF16)
        mixed = jnp.dot(wm, vb[:, gs], preferred_element_type=F32) + bst_ref[:, g:g + 1]
        cm_ref[:, gs] = (u[:, gs] * mixed).astype(cm_ref.dtype)


def _cmlp_prompt(uvg, w_s, b_s, g_v, nb, seq):
    q = CMLP_CHUNK
    nc = seq // q
    body = functools.partial(_cmlp_prompt_body, n_chunks=nc)
    return pl.pallas_call(
        body,
        grid=(nb * nc,),
        in_specs=[pl.BlockSpec((q, CMLP_WIDTH), lambda i: (i, 0)),
                  pl.BlockSpec((q, CMLP_WIDTH), lambda i: (i, 1)),
                  pl.BlockSpec((CMLP_GROUPS, q, q), lambda i: (0, 0, 0)),
                  pl.BlockSpec((q, CMLP_GROUPS), lambda i: (0, 0)),
                  pl.BlockSpec((1, CMLP_WIDTH), lambda i: (0, 0))],
        out_specs=[pl.BlockSpec((q, CMLP_WIDTH), lambda i: (i, 0)),
                   pl.BlockSpec((1, q, CMLP_WIDTH), lambda i: (i // nc, 0, 0))],
        out_shape=[jax.ShapeDtypeStruct((nb * seq, CMLP_WIDTH), BF16),
                   jax.ShapeDtypeStruct((nb, q, CMLP_WIDTH), F32)],
        compiler_params=_params(1, 32),
        name="cmlp_prompt",
    )(uvg, uvg, w_s, b_s.T, g_v.reshape(1, CMLP_WIDTH))


def _cmlp_sample_body(u_ref, v_ref, wrow_ref, brow_ref, gv_ref, cm_ref, vs_ref):
    steps = u_ref.shape[0]
    vn = [_vnorm(v_ref[t], gv_ref[...]) for t in range(steps)]
    for t in range(steps):
        vs_ref[t] = vn[t]
        mixed = brow_ref[t:t + 1, :] + wrow_ref[t, 0:1, :] * vn[0]
        for s in range(1, t + 1):
            mixed = mixed + wrow_ref[t, s:s + 1, :] * vn[s]
        cm_ref[t] = (jax.nn.gelu(u_ref[t]) * mixed).astype(cm_ref.dtype)


def _cmlp_sample(uvg_t, w_s, b_s, g_v, tb=32):
    steps, nb, _ = uvg_t.shape
    wrow = jnp.repeat(jnp.transpose(w_s[:, :steps, :steps], (1, 2, 0)), CMLP_GROUP_DIM, axis=-1)
    brow = jnp.repeat(b_s[:, :steps].T, CMLP_GROUP_DIM, axis=-1)
    return pl.pallas_call(
        _cmlp_sample_body,
        grid=(nb // tb,),
        in_specs=[pl.BlockSpec((steps, tb, CMLP_WIDTH), lambda i: (0, i, 0)),
                  pl.BlockSpec((steps, tb, CMLP_WIDTH), lambda i: (0, i, 1)),
                  pl.BlockSpec((steps, steps, CMLP_WIDTH), lambda i: (0, 0, 0)),
                  pl.BlockSpec((steps, CMLP_WIDTH), lambda i: (0, 0)),
                  pl.BlockSpec((1, CMLP_WIDTH), lambda i: (0, 0))],
        out_specs=[pl.BlockSpec((steps, tb, CMLP_WIDTH), lambda i: (0, i, 0)),
                   pl.BlockSpec((steps, tb, CMLP_WIDTH), lambda i: (0, i, 0))],
        out_shape=[jax.ShapeDtypeStruct((steps, nb, CMLP_WIDTH), BF16),
                   jax.ShapeDtypeStruct((steps, nb, CMLP_WIDTH), F32)],
        compiler_params=_params(1, 32),
        name="cmlp_sample",
    )(uvg_t, uvg_t, wrow, brow, g_v.reshape(1, CMLP_WIDTH))


def _layer(x, ple, lw, mixers, big):
    tm = 1024 if big else 512
    h = _rownorm(x, lw["g_mix"], BF16)
    zx = _fused_mm([h], [(0, lw["w_in"], 0)], [], _epi_first,
                   n=OFF_XBC, tm=tm, tn=1024, out_dtype=F32, name="in_proj_zx")
    dtp = _fused_mm([h], [(0, lw["w_in"], OFF_XBC // LANES)], [], _epi_first,
                    n=LANES, tm=tm, tn=LANES, out_dtype=F32, name="in_proj_dt")
    uvg = _fused_mm([h], [(0, lw["w_tail"], 0)], [], _epi_first,
                    n=4 * D_MODEL, tm=tm, tn=1024, out_dtype=F32, name="in_proj_uvg")
    yn, cm, states = mixers(zx, dtp, uvg)
    tn = 512
    merged = _fused_mm([yn, cm], [(0, lw["w_br_a"], 0), (1, lw["w_br_b"], 0)],
                       [(uvg, 2 * D_MODEL // tn), (uvg, 3 * D_MODEL // tn)], _epi_gate_merge,
                       n=D_MODEL, tm=512, tn=tn, out_dtype=BF16, name="branch_merge")
    x = _fused_mm([merged], [(0, lw["w_out"], 0)], [(x, 0)], _epi_residual,
                  n=D_MODEL, tm=tm, tn=tn, out_dtype=F32, name="out_proj")
    hn = _rownorm(x, lw["g_ffn"], BF16)
    act = _fused_mm([hn], [(0, lw["w_gate_up"], 0), (0, lw["w_gate_up"], FFN_HIDDEN // tn)], [],
                    _epi_swiglu, n=FFN_HIDDEN, tm=tm, tn=tn, out_dtype=BF16, name="ffn_up")
    x = _fused_mm([act], [(0, lw["w_down"], 0)], [(x, 0)], _epi_residual,
                  n=D_MODEL, tm=512, tn=tn, out_dtype=F32, name="ffn_down")
    hp = _rownorm(x, lw["g_ple"], BF16)
    x = _fused_mm([hp, ple], [(0, lw["w_ple_gate"], 0), (1, lw["w_ple"], 0)], [(x, 0)], _epi_ple,
                  n=D_MODEL, tm=tm, tn=tn, out_dtype=F32, name="ple")
    return x, states


def kernel(x_prompt, x_sample, p_prompt, p_sample, state_ssd, state_conv, g_mix, w_in, conv_w, conv_b,
           dt_bias, a_log, d_skip, g_ssd, w_br_a, g_v, w_s, b_s, w_br_b, w_out, g_ffn, w_gate_up, w_down,
           g_ple, w_ple_gate, w_ple, g_final):
    nbp, seq, _ = x_prompt.shape
    nbs, steps, _ = x_sample.shape
    assert seq % SSD_CHUNK == 0 and seq % CMLP_CHUNK == 0 and steps <= SSD_CHUNK

    w_tail = w_in[:, :, OFF_DT:].astype(BF16)
    head_rows = lax.broadcasted_iota(jnp.int32, (LANES, SSD_INNER), 0)
    head_cols = lax.broadcasted_iota(jnp.int32, (LANES, SSD_INNER), 1) // SSD_HEADDIM
    e3 = jnp.tile((head_rows == head_cols).astype(BF16), (3, 1))
    pad_heads = lambda v: jnp.pad(v, (0, LANES - SSD_HEADS)).reshape(1, LANES)

    xp = x_prompt.reshape(nbp * seq, D_MODEL)
    xs = jnp.transpose(x_sample, (1, 0, 2)).reshape(steps * nbs, D_MODEL)
    outs = {k: [] for k in ("ssd_p", "conv_p", "v_p", "ssd_s", "conv_s", "v_s")}
    for i in range(DEPTH):
        lw = dict(g_mix=g_mix[i], w_in=w_in[i], w_tail=w_tail[i], w_br_a=w_br_a[i], w_br_b=w_br_b[i],
                  w_out=w_out[i], g_ffn=g_ffn[i], w_gate_up=w_gate_up[i], w_down=w_down[i],
                  g_ple=g_ple[i], w_ple_gate=w_ple_gate[i], w_ple=w_ple[i])
        bias, alog = pad_heads(dt_bias[i]), pad_heads(a_log[i])
        dsk_w = jnp.repeat(d_skip[i], SSD_HEADDIM).reshape(1, SSD_INNER)
        gs = g_ssd[i].reshape(1, SSD_INNER)

        def prompt_mixers(zx, dtp, uvg, i=i, bias=bias, alog=alog, dsk_w=dsk_w, gs=gs):
            xc = _conv_prompt(zx, conv_w[i], conv_b[i], nbp, seq)
            yn, st = _ssd_prompt(xc, zx, dtp, bias, alog, dsk_w, gs, e3, nbp, seq)
            cm, vst = _cmlp_prompt(uvg, w_s[i], b_s[i], g_v[i], nbp, seq)
            conv_new = zx.reshape(nbp, seq, OFF_XBC)[:, seq - (CONV_W - 1):, OFF_Z:]
            return yn, cm, (st.reshape(nbp, SSD_HEADS, SSD_HEADDIM, SSD_STATE), conv_new, vst)

        def sample_mixers(zx, dtp, uvg, i=i, bias=bias, alog=alog, dsk_w=dsk_w, gs=gs):
            zx_t = zx.reshape(steps, nbs, OFF_XBC)
            xbc_t = zx_t[:, :, OFF_Z:]
            xc_t = _conv_sample(xbc_t, jnp.transpose(state_conv[i], (1, 0, 2)), conv_w[i], conv_b[i])
            seq_major = lambda t, w: jnp.transpose(t.reshape(steps, nbs, w), (1, 0, 2)).reshape(nbs * steps, w)
            yn_b, st = _ssd_sample(seq_major(xc_t, CONV_DIM), seq_major(zx_t[:, :, :OFF_Z], OFF_Z),
                                   seq_major(dtp, LANES),
                                   state_ssd[i].reshape(nbs, SSD_INNER, SSD_STATE),
                                   bias, alog, dsk_w, gs, e3, steps)
            yn = jnp.transpose(yn_b.reshape(nbs, steps, SSD_INNER), (1, 0, 2)).reshape(steps * nbs, SSD_INNER)
            cm_t, vn_t = _cmlp_sample(uvg.reshape(steps, nbs, 4 * D_MODEL), w_s[i], b_s[i], g_v[i])
            conv_new = jnp.transpose(xbc_t[steps - (CONV_W - 1):], (1, 0, 2))
            return (yn, cm_t.reshape(steps * nbs, CMLP_WIDTH),
                    (st.reshape(nbs, SSD_HEADS, SSD_HEADDIM, SSD_STATE), conv_new,
                     jnp.transpose(vn_t, (1, 0, 2))))

        xp, (hp_, cp_, vp_) = _layer(xp, p_prompt[i].reshape(nbp * seq, PLE_DIM), lw, prompt_mixers, True)
        ple_s = jnp.transpose(p_sample[i], (1, 0, 2)).reshape(steps * nbs, PLE_DIM)
        xs, (hs_, cs_, vs_) = _layer(xs, ple_s, lw, sample_mixers, False)
        outs["ssd_p"].append(hp_); outs["conv_p"].append(cp_); outs["v_p"].append(vp_)
        outs["ssd_s"].append(hs_); outs["conv_s"].append(cs_); outs["v_s"].append(vs_)

    y_prompt = _rownorm(xp, g_final, F32).reshape(nbp, seq, D_MODEL)
    y_sample = jnp.transpose(_rownorm(xs, g_final, F32).reshape(steps, nbs, D_MODEL), (1, 0, 2))
    return (y_prompt, y_sample, jnp.stack(outs["ssd_p"]), jnp.stack(outs["conv_p"]), jnp.stack(outs["v_p"]),
            jnp.stack(outs["ssd_s"]), jnp.stack(outs["conv_s"]), jnp.stack(outs["v_s"]))
```

```python
import functools

import jax
import jax.numpy as jnp
from jax import lax
from jax.experimental import pallas as pl
from jax.experimental.pallas import tpu as pltpu

F32 = jnp.float32
BF16 = jnp.bfloat16

D_MODEL = 2048
DEPTH = 4
PLE_DIM = 256
RMS_EPS = 1e-6
SSD_INNER = 4096
SSD_HEADDIM = 64
SSD_HEADS = 64
SSD_GROUPS = 8
SSD_HPG = 8
SSD_STATE = 128
SSD_CHUNK = 128
CONV_W = 4
CONV_DIM = 6144
CMLP_WIDTH = 2048
CMLP_GROUPS = 16
CMLP_GROUP_DIM = 128
CMLP_CHUNK = 128
FFN_HIDDEN = 5632
OFF_Z = SSD_INNER
OFF_XBC = OFF_Z + CONV_DIM
OFF_DT = OFF_XBC + SSD_HEADS
IN_WIDTH = OFF_DT + 4 * D_MODEL
GROUP_COLS = SSD_HPG * SSD_HEADDIM
LANES = 128
HALF_LANES = LANES // 2
VMEM_LIMIT_MB = 58


def _params(n_axes, vmem_mb=VMEM_LIMIT_MB):
    return pltpu.CompilerParams(dimension_semantics=("arbitrary",) * n_axes,
                                vmem_limit_bytes=vmem_mb * 2 ** 20)


def _softplus(x):
    return jnp.maximum(x, 0.0) + jnp.log1p(jnp.exp(-jnp.abs(x)))


def _split(q, parts):
    out, rem = [], q
    for _ in range(parts):
        piece = rem.astype(BF16)
        out.append(piece)
        rem = rem - piece.astype(F32)
    return jnp.concatenate(out, axis=1)


def _rownorm_body(x_ref, g_ref, o_ref):
    xf = x_ref[...]
    r = lax.rsqrt(jnp.mean(xf * xf, axis=-1, keepdims=True) + RMS_EPS)
    o_ref[...] = ((xf * r) * g_ref[...]).astype(o_ref.dtype)


def _rownorm(x, g, out_dtype, tr=512):
    m, d = x.shape
    return pl.pallas_call(
        _rownorm_body,
        grid=(m // tr,),
        in_specs=[pl.BlockSpec((tr, d), lambda i: (i, 0)),
                  pl.BlockSpec((1, d), lambda i: (0, 0))],
        out_specs=pl.BlockSpec((tr, d), lambda i: (i, 0)),
        out_shape=jax.ShapeDtypeStruct((m, d), out_dtype),
        compiler_params=_params(1, 32),
        name="rownorm",
    )(x, g.reshape(1, d))


def _fused_mm_body(*refs, n_acts, dots, n_extras, epi):
    a_refs = refs[:n_acts]
    pos = n_acts
    w_refs = []
    for _, shifted in dots:
        w_refs.append(refs[pos:pos + (2 if shifted else 1)])
        pos += 2 if shifted else 1
    e_refs = refs[pos:pos + n_extras]
    o_ref = refs[pos + n_extras]
    scr = refs[pos + n_extras + 1:]

    @pl.when(pl.program_id(1) == 0)
    def _():
        for (_, shifted), wr, s in zip(dots, w_refs, scr):
            if shifted:
                cat = jnp.concatenate([wr[0][...], wr[1][...]], axis=1)
                width = cat.shape[1]
                s[...] = pltpu.roll(cat, width - HALF_LANES, axis=1)[:, :width - LANES].astype(BF16)
            else:
                s[...] = wr[0][...].astype(BF16)

    acts = [a[...].astype(BF16) for a in a_refs]
    accs = [jnp.dot(acts[ai], s[...], preferred_element_type=F32) for (ai, _), s in zip(dots, scr)]
    o_ref[...] = epi(accs, [e[...] for e in e_refs]).astype(o_ref.dtype)


def _fused_mm(acts, weights, extras, epi, *, layer, n, tm, tn, out_dtype, name):
    m = acts[0].shape[-2]
    tm = min(tm, m)
    in_specs, args, scratch = [], [], []
    for a in acts:
        if a.ndim == 3:
            in_specs.append(pl.BlockSpec((None, tm, a.shape[2]), lambda j, i: (layer, i, 0)))
        else:
            in_specs.append(pl.BlockSpec((tm, a.shape[1]), lambda j, i: (i, 0)))
        args.append(a)
    for ai, w, off, shifted in weights:
        k = acts[ai].shape[-1]
        in_specs.append(pl.BlockSpec((None, k, tn), lambda j, i, off=off: (layer, 0, j + off)))
        args.append(w)
        if shifted:
            per = tn // LANES
            in_specs.append(pl.BlockSpec((None, k, LANES), lambda j, i, off=off, per=per: (layer, 0, (j + off + 1) * per)))
            args.append(w)
        scratch.append(pltpu.VMEM((k, tn), BF16))
    for e, off in extras:
        in_specs.append(pl.BlockSpec((tm, tn), lambda j, i, off=off: (i, j + off)))
        args.append(e)
    body = functools.partial(_fused_mm_body, n_acts=len(acts), dots=tuple((w[0], w[3]) for w in weights),
                             n_extras=len(extras), epi=epi)
    return pl.pallas_call(
        body,
        grid=(n // tn, m // tm),
        in_specs=in_specs,
        out_specs=pl.BlockSpec((tm, tn), lambda j, i: (i, j)),
        out_shape=jax.ShapeDtypeStruct((m, n), out_dtype),
        scratch_shapes=scratch,
        compiler_params=_params(2),
        name=name,
    )(*args)


def _epi_first(accs, extras):
    return accs[0]


def _epi_residual(accs, extras):
    return extras[0] + accs[0]


def _epi_gate_merge(accs, extras):
    return jax.nn.sigmoid(extras[0]) * accs[0] + jax.nn.sigmoid(extras[1]) * accs[1]


def _epi_swiglu(accs, extras):
    return jax.nn.silu(accs[0]) * accs[1]


def _epi_ple(accs, extras):
    return extras[0] + jax.nn.sigmoid(accs[0]) * accs[1]


def _conv_prompt_body(x_ref, w_ref, b_ref, o_ref):
    x = x_ref[...]
    rows = lax.broadcasted_iota(jnp.int32, x.shape, 0)
    acc = b_ref[...] + w_ref[CONV_W - 1:CONV_W, :] * x
    for j in range(1, CONV_W):
        shifted = jnp.where(rows >= j, pltpu.roll(x, j, axis=0), 0.0)
        acc = acc + w_ref[CONV_W - 1 - j:CONV_W - j, :] * shifted
    o_ref[...] = acc * jax.nn.sigmoid(acc)


def _conv_prompt(zx, conv_w, conv_b, layer, nb, seq, tc=512):
    col0 = OFF_Z // tc
    return pl.pallas_call(
        _conv_prompt_body,
        grid=(nb, CONV_DIM // tc),
        in_specs=[pl.BlockSpec((seq, tc), lambda b, c: (b, c + col0)),
                  pl.BlockSpec((None, CONV_W, tc), lambda b, c: (layer, 0, c)),
                  pl.BlockSpec((1, tc), lambda b, c: (0, c))],
        out_specs=pl.BlockSpec((seq, tc), lambda b, c: (b, c)),
        out_shape=jax.ShapeDtypeStruct((nb * seq, CONV_DIM), F32),
        compiler_params=_params(2, 40),
        name="conv_prompt",
    )(zx, conv_w, conv_b.reshape(1, CONV_DIM))


def _conv_sample_body(x_ref, s_ref, w_ref, b_ref, o_ref):
    steps = x_ref.shape[0]
    xpad = [s_ref[k] for k in range(CONV_W - 1)] + [x_ref[t] for t in range(steps)]
    for t in range(steps):
        acc = b_ref[...] + w_ref[0:1, :] * xpad[t]
        for k in range(1, CONV_W):
            acc = acc + w_ref[k:k + 1, :] * xpad[t + k]
        o_ref[t] = acc * jax.nn.sigmoid(acc)


def _conv_sample(xbc_t, state_t, conv_w, conv_b, layer, tc=1024):
    steps, nb, _ = xbc_t.shape
    return pl.pallas_call(
        _conv_sample_body,
        grid=(CONV_DIM // tc,),
        in_specs=[pl.BlockSpec((steps, nb, tc), lambda c: (0, 0, c)),
                  pl.BlockSpec((None, CONV_W - 1, nb, tc), lambda c: (layer, 0, 0, c)),
                  pl.BlockSpec((None, CONV_W, tc), lambda c: (layer, 0, c)),
                  pl.BlockSpec((1, tc), lambda c: (0, c))],
        out_specs=pl.BlockSpec((steps, nb, tc), lambda c: (0, 0, c)),
        out_shape=jax.ShapeDtypeStruct((steps, nb, CONV_DIM), F32),
        compiler_params=_params(1, 32),
        name="conv_sample",
    )(xbc_t, state_t, conv_w, conv_b.reshape(1, CONV_DIM))


HEADS_PER_DOT = 4
QUAD_COLS = HEADS_PER_DOT * SSD_HEADDIM


def _gated_norm(y2, g):
    r = lax.rsqrt(jnp.mean(y2 * y2, axis=-1, keepdims=True) + RMS_EPS)
    return (y2 * r) * g


def _ssd_prompt_body(xc_ref, z_ref, dtp_ref, bias_ref, alog_ref, dsk_ref, g_ref, e2_ref,
                     yn_ref, st_ref, s_scr, y_scr, *, n_chunks):
    c = pl.program_id(1)
    q = SSD_CHUNK

    @pl.when(c == 0)
    def _():
        s_scr[...] = jnp.zeros_like(s_scr)

    lane = lax.broadcasted_iota(jnp.int32, (q, LANES), 1)
    row = lax.broadcasted_iota(jnp.int32, (q, LANES), 0)
    head_ok = lane < SSD_HEADS
    dt = jnp.where(head_ok, _softplus(dtp_ref[...] + bias_ref[...]), 0.0)
    a = -jnp.exp(alog_ref[...])
    dta = dt * a
    tri = (row >= lane).astype(BF16)
    cum3 = jnp.dot(tri, _split(dta, 3), preferred_element_type=F32)
    cum = cum3[:, :LANES] + cum3[:, LANES:2 * LANES] + cum3[:, 2 * LANES:]
    cum_t = cum.T
    dt_t = dt.T
    last = cum[q - 1:q, :]
    ecum = jnp.exp(cum)
    dtw = jnp.where(head_ok, dt * jnp.exp(last - cum), 0.0)
    stack = jnp.concatenate([dtw, ecum], axis=0)
    wide = jnp.dot(_split(stack, 2), e2_ref[...], preferred_element_type=F32)
    dtw_w, ecum_w = wide[:q], wide[q:]
    cdec_w = ecum_w[q - 1:q, :]

    xs = xc_ref[:, :SSD_INNER]
    xs_b = xs.astype(BF16)
    xdd_b = (xs * dtw_w).astype(BF16)
    causal = row >= lane
    lane_q = lax.broadcasted_iota(jnp.int32, (q, QUAD_COLS), 1) // SSD_HEADDIM
    head_masks = [(lane_q == k).astype(BF16) for k in range(HEADS_PER_DOT)]
    for g in range(SSD_GROUPS):
        gs = slice(g * GROUP_COLS, (g + 1) * GROUP_COLS)
        b_g = xc_ref[:, SSD_INNER + g * SSD_STATE:SSD_INNER + (g + 1) * SSD_STATE]
        c_g = xc_ref[:, SSD_INNER + (SSD_GROUPS + g) * SSD_STATE:SSD_INNER + (SSD_GROUPS + g + 1) * SSD_STATE]
        c_b = c_g.astype(BF16)
        cb = lax.dot_general(c_b, b_g.astype(BF16), (((1,), (1,)), ((), ())), preferred_element_type=F32)
        s_g = s_scr[g]
        y_off = jnp.dot(c_b, s_g.astype(BF16), preferred_element_type=F32)
        y_quads = []
        for quad in range(SSD_HPG // HEADS_PER_DOT):
            h0 = g * SSD_HPG + quad * HEADS_PER_DOT
            m_heads = []
            for k in range(HEADS_PER_DOT):
                h = h0 + k
                diff = cum[:, h:h + 1] - cum_t[h:h + 1, :]
                seg = jnp.exp(jnp.where(causal, diff, -jnp.inf))
                m_heads.append((cb * seg * dt_t[h:h + 1, :]).astype(BF16))
            xq = xs_b[:, h0 * SSD_HEADDIM:h0 * SSD_HEADDIM + QUAD_COLS]
            rhs = jnp.concatenate([xq * head_masks[k] for k in range(HEADS_PER_DOT)], axis=0)
            y_quads.append(jnp.dot(jnp.concatenate(m_heads, axis=1), rhs, preferred_element_type=F32))
        y_g = jnp.concatenate(y_quads, axis=1) + y_off * ecum_w[:, gs]
        upd = jnp.dot(b_g.T.astype(BF16), xdd_b[:, gs], preferred_element_type=F32)
        s_new = s_g * cdec_w[:, gs] + upd
        s_scr[g] = s_new

        @pl.when(c == n_chunks - 1)
        def _(s_new=s_new, gs=gs):
            st_ref[0, gs, :] = s_new.T

        zg = z_ref[:, gs]
        y_scr[:, gs] = (y_g + xs[:, gs] * dsk_ref[:, gs]) * (zg * jax.nn.sigmoid(zg))
    yn_ref[...] = _gated_norm(y_scr[...], g_ref[...]).astype(yn_ref.dtype)


def _ssd_prompt(xc, zx, dtp, bias, alog, dsk_w, g_ssd, e2, nb, seq):
    q = SSD_CHUNK
    nc = seq // q
    body = functools.partial(_ssd_prompt_body, n_chunks=nc)
    const = lambda b, c: (0, 0)
    return pl.pallas_call(
        body,
        grid=(nb, nc),
        in_specs=[pl.BlockSpec((q, CONV_DIM), lambda b, c: (b * nc + c, 0)),
                  pl.BlockSpec((q, SSD_INNER), lambda b, c: (b * nc + c, 0)),
                  pl.BlockSpec((q, LANES), lambda b, c: (b * nc + c, 0)),
                  pl.BlockSpec((1, LANES), const),
                  pl.BlockSpec((1, LANES), const),
                  pl.BlockSpec((1, SSD_INNER), const),
                  pl.BlockSpec((1, SSD_INNER), const),
                  pl.BlockSpec((2 * LANES, SSD_INNER), const)],
        out_specs=[pl.BlockSpec((q, SSD_INNER), lambda b, c: (b * nc + c, 0)),
                   pl.BlockSpec((1, SSD_INNER, SSD_STATE), lambda b, c: (b, 0, 0))],
        out_shape=[jax.ShapeDtypeStruct((nb * seq, SSD_INNER), BF16),
                   jax.ShapeDtypeStruct((nb, SSD_INNER, SSD_STATE), F32)],
        scratch_shapes=[pltpu.VMEM((SSD_GROUPS, SSD_STATE, GROUP_COLS), F32),
                        pltpu.VMEM((q, SSD_INNER), F32)],
        compiler_params=_params(2, 48),
        name="ssd_prompt",
    )(xc, zx, dtp, bias, alog, dsk_w, g_ssd, e2)


SEQ_PER_STEP = 4


def _ssd_sample_body(xc_ref, z_ref, dtp_ref, h0_ref, bias_ref, alog_ref, dsk_ref, g_ref, e3_ref, *rest, steps):
    yn_ref, hn_ref = rest[-2:]
    rows_n = SEQ_PER_STEP * steps
    lane = lax.broadcasted_iota(jnp.int32, (rows_n, LANES), 1)
    row = lax.broadcasted_iota(jnp.int32, (rows_n, LANES), 0)
    tpos = row % steps
    head_ok = lane < SSD_HEADS

    def shift(v, j, tp):
        return jnp.where(tp >= j, pltpu.roll(v, j, axis=0), 0.0)

    dt = jnp.where(head_ok, _softplus(dtp_ref[...] + bias_ref[...]), 0.0)
    a = -jnp.exp(alog_ref[...])
    dta = dt * a
    cum = dta
    for j in range(1, steps):
        cum = cum + shift(dta, j, tpos)
    last = jnp.zeros_like(cum)
    for k in range(steps):
        back = steps - 1 - k
        src = cum if back == 0 else pltpu.roll(cum, rows_n - back, axis=0)
        last = jnp.where(tpos == k, src, last)
    ecum = jnp.exp(cum)
    dte = jnp.where(head_ok, jnp.exp(last - cum), 0.0)
    cdec = jnp.exp(last)

    b_all = xc_ref[:, SSD_INNER:SSD_INNER + SSD_GROUPS * SSD_STATE]
    c_all = xc_ref[:, SSD_INNER + SSD_GROUPS * SSD_STATE:]
    row_w = lax.broadcasted_iota(jnp.int32, b_all.shape, 0)
    tpos_w = row_w % steps
    coefs = []
    for j in range(steps):
        prod = c_all * (b_all if j == 0 else shift(b_all, j, tpos_w))
        cb = jnp.zeros((rows_n, LANES), F32)
        for g in range(SSD_GROUPS):
            cbg = jnp.sum(prod[:, g * SSD_STATE:(g + 1) * SSD_STATE], axis=1, keepdims=True)
            cb = jnp.where((lane // SSD_HPG) == g, cbg, cb)
        if j == 0:
            coefs.append(jnp.where(head_ok, cb, 0.0))
        else:
            seg = jnp.exp(cum - shift(cum, j, tpos))
            coefs.append(jnp.where(head_ok & (tpos >= j), cb * seg, 0.0))
    stack = jnp.concatenate([dt, dte, ecum, cdec] + coefs, axis=0)
    wide = jnp.dot(_split(stack, 3), e3_ref[...], preferred_element_type=F32)
    dt_w, dte_w = wide[:rows_n], wide[rows_n:2 * rows_n]
    ecum_w, cdec_w = wide[2 * rows_n:3 * rows_n], wide[3 * rows_n:4 * rows_n]

    xs = xc_ref[:, :SSD_INNER]
    xdt = xs * dt_w
    y = wide[4 * rows_n:5 * rows_n] * xdt
    for j in range(1, steps):
        y = y + wide[(4 + j) * rows_n:(5 + j) * rows_n] * pltpu.roll(xdt, j, axis=0)
    xdd = xdt * dte_w

    row_g = lax.broadcasted_iota(jnp.int32, (rows_n, GROUP_COLS), 0)
    ones_b = jnp.ones((rows_n, SSD_STATE), BF16)
    c_bf = c_all.astype(BF16)
    b_bf = b_all.astype(BF16)
    y_off_groups = []
    for g in range(SSD_GROUPS):
        gs = slice(g * GROUP_COLS, (g + 1) * GROUP_COLS)
        ns = slice(g * SSD_STATE, (g + 1) * SSD_STATE)
        y_off = jnp.zeros((rows_n, GROUP_COLS), F32)
        for i in range(SEQ_PER_STEP):
            mine = (row_g // steps) == i
            h0g = h0_ref[i, gs, :]
            yo = lax.dot_general(c_bf[:, ns], h0g.astype(BF16), (((1,), (1,)), ((), ())),
                                 preferred_element_type=F32)
            y_off = jnp.where(mine, yo, y_off)
            x_i = jnp.where(mine, xdd[:, gs], 0.0).astype(BF16)
            upd = lax.dot_general(x_i, b_bf[:, ns], (((0,), (0,)), ((), ())),
                                  preferred_element_type=F32)
            cd = cdec_w[i * steps:i * steps + 1, gs]
            cd_hi = cd.astype(BF16).astype(F32)
            cd_mid = (cd - cd_hi).astype(BF16).astype(F32)
            cd_lo = cd - cd_hi - cd_mid
            cd_rows = jnp.where(row_g == 0, cd_hi,
                                jnp.where(row_g == 1, cd_mid,
                                          jnp.where(row_g == 2, cd_lo, 0.0))).astype(BF16)
            cd_col = lax.dot_general(cd_rows, ones_b, (((0,), (0,)), ((), ())),
                                     preferred_element_type=F32)
            hn_ref[i, gs, :] = h0g * cd_col + upd
        y_off_groups.append(y_off)
    y = y + jnp.concatenate(y_off_groups, axis=1) * ecum_w
    zz = z_ref[...]
    y2 = (y + xs * dsk_ref[...]) * (zz * jax.nn.sigmoid(zz))
    yn_ref[...] = _gated_norm(y2, g_ref[...]).astype(yn_ref.dtype)


def _ssd_sample(xc, z, dtp, h_all, layer, new_states, bias, alog, dsk_w, g_ssd, e3, steps):
    nb = h_all.shape[1]
    rows_n = SEQ_PER_STEP * steps
    body = functools.partial(_ssd_sample_body, steps=steps)
    const = lambda i: (0, 0)
    state_spec = pl.BlockSpec((None, SEQ_PER_STEP, SSD_INNER, SSD_STATE), lambda i: (layer, i, 0, 0))
    in_specs = [pl.BlockSpec((rows_n, CONV_DIM), lambda i: (i, 0)),
                pl.BlockSpec((rows_n, SSD_INNER), lambda i: (i, 0)),
                pl.BlockSpec((rows_n, LANES), lambda i: (i, 0)),
                state_spec,
                pl.BlockSpec((1, LANES), const),
                pl.BlockSpec((1, LANES), const),
                pl.BlockSpec((1, SSD_INNER), const),
                pl.BlockSpec((1, SSD_INNER), const),
                pl.BlockSpec((3 * LANES, SSD_INNER), const)]
    args = [xc, z, dtp, h_all, bias, alog, dsk_w, g_ssd, e3]
    aliases = {}
    if new_states is not None:
        in_specs.append(pl.BlockSpec(memory_space=pl.ANY))
        args.append(new_states)
        aliases = {len(args) - 1: 1}
    return pl.pallas_call(
        body,
        grid=(nb // SEQ_PER_STEP,),
        in_specs=in_specs,
        out_specs=[pl.BlockSpec((rows_n, SSD_INNER), lambda i: (i, 0)), state_spec],
        out_shape=[jax.ShapeDtypeStruct((nb * steps, SSD_INNER), BF16),
                   jax.ShapeDtypeStruct(h_all.shape, F32)],
        input_output_aliases=aliases,
        compiler_params=_params(1, 56),
        name="ssd_sample",
    )(*args)


def _vnorm(v, g):
    v = jax.nn.gelu(v)
    r = lax.rsqrt(jnp.mean(v * v, axis=-1, keepdims=True) + RMS_EPS)
    return (v * r) * g


def _cmlp_prompt_body(u_ref, v_ref, ws_ref, bst_ref, gv_ref, cm_ref, vs_ref, *, n_chunks):
    q = CMLP_CHUNK
    vn = _vnorm(v_ref[...], gv_ref[...])

    @pl.when(pl.program_id(0) % n_chunks == n_chunks - 1)
    def _():
        vs_ref[0] = vn

    u = jax.nn.gelu(u_ref[...])
    row = lax.broadcasted_iota(jnp.int32, (q, q), 0)
    col = lax.broadcasted_iota(jnp.int32, (q, q), 1)
    vb = vn.astype(BF16)
    for g in range(CMLP_GROUPS):
        gs = slice(g * CMLP_GROUP_DIM, (g + 1) * CMLP_GROUP_DIM)
        wm = jnp.where(row >= col, ws_ref[g], 0.0).astype(BF16)
        mixed = jnp.dot(wm, vb[:, gs], preferred_element_type=F32) + bst_ref[:, g:g + 1]
        cm_ref[:, gs] = (u[:, gs] * mixed).astype(cm_ref.dtype)


def _cmlp_prompt(uvg, w_s, b_s, g_v, layer, nb, seq):
    q = CMLP_CHUNK
    nc = seq // q
    body = functools.partial(_cmlp_prompt_body, n_chunks=nc)
    return pl.pallas_call(
        body,
        grid=(nb * nc,),
        in_specs=[pl.BlockSpec((q, CMLP_WIDTH), lambda i: (i, 0)),
                  pl.BlockSpec((q, CMLP_WIDTH), lambda i: (i, 1)),
                  pl.BlockSpec((None, CMLP_GROUPS, q, q), lambda i: (layer, 0, 0, 0)),
                  pl.BlockSpec((q, CMLP_GROUPS), lambda i: (0, 0)),
                  pl.BlockSpec((1, CMLP_WIDTH), lambda i: (0, 0))],
        out_specs=[pl.BlockSpec((q, CMLP_WIDTH), lambda i: (i, 0)),
                   pl.BlockSpec((1, q, CMLP_WIDTH), lambda i: (i // nc, 0, 0))],
        out_shape=[jax.ShapeDtypeStruct((nb * seq, CMLP_WIDTH), BF16),
                   jax.ShapeDtypeStruct((nb, q, CMLP_WIDTH), F32)],
        compiler_params=_params(1, 32),
        name="cmlp_prompt",
    )(uvg, uvg, w_s, b_s.T, g_v.reshape(1, CMLP_WIDTH))


def _cmlp_sample_body(u_ref, v_ref, wrow_ref, brow_ref, gv_ref, cm_ref, vs_ref):
    steps = u_ref.shape[0]
    vn = [_vnorm(v_ref[t], gv_ref[...]) for t in range(steps)]
    for t in range(steps):
        vs_ref[t] = vn[t]
        mixed = brow_ref[t:t + 1, :] + wrow_ref[t, 0:1, :] * vn[0]
        for s in range(1, t + 1):
            mixed = mixed + wrow_ref[t, s:s + 1, :] * vn[s]
        cm_ref[t] = (jax.nn.gelu(u_ref[t]) * mixed).astype(cm_ref.dtype)


def _cmlp_sample(uvg_t, w_s, b_s, g_v, tb=32):
    steps, nb, _ = uvg_t.shape
    wrow = jnp.repeat(jnp.transpose(w_s[:, :steps, :steps], (1, 2, 0)), CMLP_GROUP_DIM, axis=-1)
    brow = jnp.repeat(b_s[:, :steps].T, CMLP_GROUP_DIM, axis=-1)
    return pl.pallas_call(
        _cmlp_sample_body,
        grid=(nb // tb,),
        in_specs=[pl.BlockSpec((steps, tb, CMLP_WIDTH), lambda i: (0, i, 0)),
                  pl.BlockSpec((steps, tb, CMLP_WIDTH), lambda i: (0, i, 1)),
                  pl.BlockSpec((steps, steps, CMLP_WIDTH), lambda i: (0, 0, 0)),
                  pl.BlockSpec((steps, CMLP_WIDTH), lambda i: (0, 0)),
                  pl.BlockSpec((1, CMLP_WIDTH), lambda i: (0, 0))],
        out_specs=[pl.BlockSpec((steps, tb, CMLP_WIDTH), lambda i: (0, i, 0)),
                   pl.BlockSpec((steps, tb, CMLP_WIDTH), lambda i: (0, i, 0))],
        out_shape=[jax.ShapeDtypeStruct((steps, nb, CMLP_WIDTH), BF16),
                   jax.ShapeDtypeStruct((steps, nb, CMLP_WIDTH), F32)],
        compiler_params=_params(1, 32),
        name="cmlp_sample",
    )(uvg_t, uvg_t, wrow, brow, g_v.reshape(1, CMLP_WIDTH))


def _layer(x, ple_all, w, gains, layer, mixers, big):
    tm = 1024 if big else 512
    mm = functools.partial(_fused_mm, layer=layer)
    h = _rownorm(x, gains["g_mix"], BF16)
    zx = mm([h], [(0, w["w_in"], 0, False)], [], _epi_first,
            n=OFF_XBC, tm=tm, tn=1024, out_dtype=F32, name="in_proj_zx")
    dtp = mm([h], [(0, w["w_in"], OFF_XBC // LANES, False)], [], _epi_first,
             n=LANES, tm=tm, tn=LANES, out_dtype=F32, name="in_proj_dt")
    uvg = mm([h], [(0, w["w_in"], OFF_XBC // 1024, True)], [], _epi_first,
             n=4 * D_MODEL, tm=tm, tn=1024, out_dtype=F32, name="in_proj_uvg")
    yn, cm, states = mixers(zx, dtp, uvg)
    tn = 512
    merged = mm([yn, cm], [(0, w["w_br_a"], 0, False), (1, w["w_br_b"], 0, False)],
                [(uvg, 2 * D_MODEL // tn), (uvg, 3 * D_MODEL // tn)], _epi_gate_merge,
                n=D_MODEL, tm=512, tn=tn, out_dtype=BF16, name="branch_merge")
    x = mm([merged], [(0, w["w_out"], 0, False)], [(x, 0)], _epi_residual,
           n=D_MODEL, tm=tm, tn=1024, out_dtype=F32, name="out_proj")
    hn = _rownorm(x, gains["g_ffn"], BF16)
    act = mm([hn], [(0, w["w_gate_up"], 0, False), (0, w["w_gate_up"], FFN_HIDDEN // tn, False)], [],
             _epi_swiglu, n=FFN_HIDDEN, tm=tm, tn=tn, out_dtype=BF16, name="ffn_up")
    x = mm([act], [(0, w["w_down"], 0, False)], [(x, 0)], _epi_residual,
           n=D_MODEL, tm=512, tn=tn, out_dtype=F32, name="ffn_down")
    hp = _rownorm(x, gains["g_ple"], BF16)
    x = mm([hp, ple_all], [(0, w["w_ple_gate"], 0, False), (1, w["w_ple"], 0, False)], [(x, 0)], _epi_ple,
           n=D_MODEL, tm=tm, tn=1024, out_dtype=F32, name="ple")
    return x, states


def kernel(x_prompt, x_sample, p_prompt, p_sample, state_ssd, state_conv, g_mix, w_in, conv_w, conv_b,
           dt_bias, a_log, d_skip, g_ssd, w_br_a, g_v, w_s, b_s, w_br_b, w_out, g_ffn, w_gate_up, w_down,
           g_ple, w_ple_gate, w_ple, g_final):
    nbp, seq, _ = x_prompt.shape
    nbs, steps, _ = x_sample.shape
    assert seq % SSD_CHUNK == 0 and seq % CMLP_CHUNK == 0 and steps <= SSD_CHUNK

    head_rows = lax.broadcasted_iota(jnp.int32, (LANES, SSD_INNER), 0)
    head_cols = lax.broadcasted_iota(jnp.int32, (LANES, SSD_INNER), 1) // SSD_HEADDIM
    expand = (head_rows == head_cols).astype(BF16)
    e2, e3 = jnp.tile(expand, (2, 1)), jnp.tile(expand, (3, 1))
    pad_heads = lambda v: jnp.pad(v, (0, LANES - SSD_HEADS)).reshape(1, LANES)
    w = dict(w_in=w_in, w_br_a=w_br_a, w_br_b=w_br_b, w_out=w_out, w_gate_up=w_gate_up, w_down=w_down,
             w_ple_gate=w_ple_gate, w_ple=w_ple)

    xp = x_prompt.reshape(nbp * seq, D_MODEL)
    xs = jnp.transpose(x_sample, (1, 0, 2)).reshape(steps * nbs, D_MODEL)
    ple_p = p_prompt.reshape(DEPTH, nbp * seq, PLE_DIM)
    ple_s = jnp.transpose(p_sample, (0, 2, 1, 3)).reshape(DEPTH, steps * nbs, PLE_DIM)
    conv_state_t = jnp.transpose(state_conv, (0, 2, 1, 3))
    h_all = state_ssd.reshape(DEPTH, nbs, SSD_INNER, SSD_STATE)
    new_states = None
    outs = {k: [] for k in ("ssd_p", "conv_p", "v_p", "conv_s", "v_s")}
    for i in range(DEPTH):
        gains = dict(g_mix=g_mix[i], g_ffn=g_ffn[i], g_ple=g_ple[i])
        bias, alog = pad_heads(dt_bias[i]), pad_heads(a_log[i])
        dsk_w = jnp.repeat(d_skip[i], SSD_HEADDIM).reshape(1, SSD_INNER)
        gs = g_ssd[i].reshape(1, SSD_INNER)

        def prompt_mixers(zx, dtp, uvg, i=i, bias=bias, alog=alog, dsk_w=dsk_w, gs=gs):
            xc = _conv_prompt(zx, conv_w, conv_b[i], i, nbp, seq)
            yn, st = _ssd_prompt(xc, zx, dtp, bias, alog, dsk_w, gs, e2, nbp, seq)
            cm, vst = _cmlp_prompt(uvg, w_s, b_s[i], g_v[i], i, nbp, seq)
            conv_new = zx.reshape(nbp, seq, OFF_XBC)[:, seq - (CONV_W - 1):, OFF_Z:]
            return yn, cm, (st.reshape(nbp, SSD_HEADS, SSD_HEADDIM, SSD_STATE), conv_new, vst)

        def sample_mixers(zx, dtp, uvg, i=i, bias=bias, alog=alog, dsk_w=dsk_w, gs=gs, new_states=new_states):
            zx_t = zx.reshape(steps, nbs, OFF_XBC)
            xbc_t = zx_t[:, :, OFF_Z:]
            xc_t = _conv_sample(xbc_t, conv_state_t, conv_w, conv_b[i], i)
            seq_major = lambda t, wd: jnp.transpose(t.reshape(steps, nbs, wd), (1, 0, 2)).reshape(nbs * steps, wd)
            yn_b, st = _ssd_sample(seq_major(xc_t, CONV_DIM), seq_major(zx_t[:, :, :OFF_Z], OFF_Z),
                                   seq_major(dtp, LANES), h_all, i, new_states,
                                   bias, alog, dsk_w, gs, e3, steps)
            yn = jnp.transpose(yn_b.reshape(nbs, steps, SSD_INNER), (1, 0, 2)).reshape(steps * nbs, SSD_INNER)
            cm_t, vn_t = _cmlp_sample(uvg.reshape(steps, nbs, 4 * D_MODEL), w_s[i], b_s[i], g_v[i])
            conv_new = jnp.transpose(xbc_t[steps - (CONV_W - 1):], (1, 0, 2))
            return (yn, cm_t.reshape(steps * nbs, CMLP_WIDTH),
                    (st, conv_new, jnp.transpose(vn_t, (1, 0, 2))))

        xp, (hp_, cp_, vp_) = _layer(xp, ple_p, w, gains, i, prompt_mixers, True)
        xs, (new_states, cs_, vs_) = _layer(xs, ple_s, w, gains, i, sample_mixers, False)
        outs["ssd_p"].append(hp_); outs["conv_p"].append(cp_); outs["v_p"].append(vp_)
        outs["conv_s"].append(cs_); outs["v_s"].append(vs_)

    y_prompt = _rownorm(xp, g_final, F32).reshape(nbp, seq, D_MODEL)
    y_sample = jnp.transpose(_rownorm(xs, g_final, F32).reshape(steps, nbs, D_MODEL), (1, 0, 2))
    ssd_s = new_states.reshape(DEPTH, nbs, SSD_HEADS, SSD_HEADDIM, SSD_STATE)
    return (y_prompt, y_sample, jnp.stack(outs["ssd_p"]), jnp.stack(outs["conv_p"]), jnp.stack(outs["v_p"]),
            ssd_s, jnp.stack(outs["conv_s"]), jnp.stack(outs["v_s"]))
```

```python
import functools
from typing import Any, NamedTuple

import jax
import jax.numpy as jnp
from jax import lax
from jax.experimental import pallas as pl
from jax.experimental.pallas import tpu as pltpu

F32 = jnp.float32
BF16 = jnp.bfloat16

D_MODEL = 2048
DEPTH = 4
PLE_DIM = 256
RMS_EPS = 1e-6
SSD_INNER = 4096
SSD_HEADDIM = 64
SSD_HEADS = 64
SSD_GROUPS = 8
SSD_HPG = 8
SSD_STATE = 128
SSD_CHUNK = 128
CONV_W = 4
CONV_DIM = 6144
BC_COLS = 2 * SSD_GROUPS * SSD_STATE
CMLP_WIDTH = 2048
CMLP_GROUPS = 16
CMLP_GROUP_DIM = 128
CMLP_CHUNK = 128
FFN_HIDDEN = 5632
OFF_Z = SSD_INNER
OFF_XBC = OFF_Z + CONV_DIM
OFF_DT = OFF_XBC + SSD_HEADS
IN_WIDTH = OFF_DT + 4 * D_MODEL
GROUP_COLS = SSD_HPG * SSD_HEADDIM
LANES = 128
SUBLANES = 8
HALF_LANES = LANES // 2
VMEM_LIMIT_MB = 58
NT_DIMS = (((1,), (1,)), ((), ()))


def _params(n_axes, vmem_mb=VMEM_LIMIT_MB):
    return pltpu.CompilerParams(dimension_semantics=("arbitrary",) * n_axes,
                                vmem_limit_bytes=vmem_mb * 2 ** 20)


def _softplus(x):
    return jnp.maximum(x, 0.0) + jnp.log1p(jnp.exp(-jnp.abs(x)))


def _split(q, parts):
    out, rem = [], q
    for _ in range(parts):
        piece = rem.astype(BF16)
        out.append(piece)
        rem = rem - piece.astype(F32)
    return jnp.concatenate(out, axis=1)


def _inv_rms(ss_parts, width):
    return lax.rsqrt(jnp.sum(ss_parts, axis=0) / width + RMS_EPS)


def _rownorm_body(x_ref, g_ref, o_ref):
    xf = x_ref[...]
    r = lax.rsqrt(jnp.mean(xf * xf, axis=-1, keepdims=True) + RMS_EPS)
    o_ref[...] = ((xf * r) * g_ref[...]).astype(o_ref.dtype)


def _rownorm(x, g, out_dtype, tr=512):
    m, d = x.shape
    return pl.pallas_call(
        _rownorm_body,
        grid=(m // tr,),
        in_specs=[pl.BlockSpec((tr, d), lambda i: (i, 0)),
                  pl.BlockSpec((1, d), lambda i: (0, 0))],
        out_specs=pl.BlockSpec((tr, d), lambda i: (i, 0)),
        out_shape=jax.ShapeDtypeStruct((m, d), out_dtype),
        compiler_params=_params(1, 32),
        name="rownorm",
    )(x, g.reshape(1, d))


def _prenorm_body(x_ref, g_ref, xg_ref, ss_ref):
    xf = x_ref[...]
    xg_ref[...] = (xf * g_ref[...]).astype(xg_ref.dtype)
    ss_ref[...] = jnp.sum(xf * xf, axis=-1, keepdims=True)


def _prenorm(x, g, tr=512):
    m, d = x.shape
    return pl.pallas_call(
        _prenorm_body,
        grid=(m // tr,),
        in_specs=[pl.BlockSpec((tr, d), lambda i: (i, 0)),
                  pl.BlockSpec((1, d), lambda i: (0, 0))],
        out_specs=[pl.BlockSpec((tr, d), lambda i: (i, 0)),
                   pl.BlockSpec((None, tr, 1), lambda i: (0, i, 0))],
        out_shape=[jax.ShapeDtypeStruct((m, d), BF16), jax.ShapeDtypeStruct((1, m, 1), F32)],
        compiler_params=_params(1, 32),
        name="prenorm",
    )(x, g.reshape(1, d))


class Dot(NamedTuple):
    act: int
    w: Any
    off: int = 0
    scaled: bool = False
    transposed: bool = False
    shifted: bool = False


def _fused_mm_body(*refs, n_acts, dots, n_extras, epi, has_ss, has_gain, emit_w):
    a_refs = refs[:n_acts]
    pos = n_acts
    w_refs = []
    for d in dots:
        w_refs.append(refs[pos:pos + (2 if d.shifted else 1)])
        pos += 2 if d.shifted else 1
    ss_ref = refs[pos] if has_ss else None
    pos += has_ss
    e_refs = refs[pos:pos + n_extras]
    pos += n_extras
    gain_ref = refs[pos] if has_gain else None
    pos += has_gain
    o_ref = refs[pos]
    pos += 1
    if has_gain:
        xg_ref, ssq_ref = refs[pos:pos + 2]
        pos += 2
    wout_refs = refs[pos:pos + len(dots)] if emit_w else ()
    pos += len(wout_refs)
    scr = refs[pos:]

    @pl.when(pl.program_id(1) == 0)
    def _():
        for d, wr, s in zip(dots, w_refs, scr):
            if d.shifted:
                keep = s.shape[0] - HALF_LANES
                s[:keep, :] = wr[0][HALF_LANES:, :].astype(BF16)
                s[keep:, :] = wr[1][...].astype(BF16)
            else:
                s[...] = wr[0][...].astype(BF16)
        for wo, s in zip(wout_refs, scr):
            wo[...] = s[...]

    acts = [a[...].astype(BF16) for a in a_refs]
    r = _inv_rms(ss_ref[...], D_MODEL) if has_ss else None
    accs = []
    for d, s in zip(dots, scr):
        if d.transposed:
            acc = lax.dot_general(acts[d.act], s[...], NT_DIMS, preferred_element_type=F32)
        else:
            acc = jnp.dot(acts[d.act], s[...], preferred_element_type=F32)
        accs.append(acc * r if d.scaled else acc)
    out = epi(accs, [e[...] for e in e_refs])
    o_ref[...] = out.astype(o_ref.dtype)
    if has_gain:
        xg_ref[...] = (out * gain_ref[...]).astype(xg_ref.dtype)
        ssq_ref[...] = jnp.sum(out * out, axis=-1, keepdims=True)


def _fused_mm(acts, dots, extras, epi, *, layer, n, tm, tn, out_dtype, name, ss=None, next_gain=None,
              emit_w=False):
    m = acts[0].shape[-2]
    tm = min(tm, m)
    in_specs, args, scratch, w_specs, w_shapes = [], [], [], [], []
    for a in acts:
        if a.ndim == 3:
            in_specs.append(pl.BlockSpec((None, tm, a.shape[2]), lambda j, i: (layer, i, 0)))
        else:
            in_specs.append(pl.BlockSpec((tm, a.shape[1]), lambda j, i: (i, 0)))
        args.append(a)
    for d in dots:
        k = acts[d.act].shape[-1]
        lead = (None,) if d.w.ndim == 3 else ()
        at = (lambda *ix: (layer,) + ix) if d.w.ndim == 3 else (lambda *ix: ix)
        if d.transposed:
            in_specs.append(pl.BlockSpec(lead + (tn, k), lambda j, i, off=d.off, at=at: at(j + off, 0)))
            args.append(d.w)
            if d.shifted:
                per = tn // HALF_LANES
                in_specs.append(pl.BlockSpec(lead + (HALF_LANES, k),
                                             lambda j, i, off=d.off, per=per, at=at: at((j + off + 1) * per, 0)))
                args.append(d.w)
            scratch.append(pltpu.VMEM((tn, k), BF16))
            w_specs.append(pl.BlockSpec((tn, k), lambda j, i: (j, 0)))
            w_shapes.append(jax.ShapeDtypeStruct((n, k), BF16))
        else:
            in_specs.append(pl.BlockSpec(lead + (k, tn), lambda j, i, off=d.off, at=at: at(0, j + off)))
            args.append(d.w)
            scratch.append(pltpu.VMEM((k, tn), BF16))
            w_specs.append(pl.BlockSpec((k, tn), lambda j, i: (0, j)))
            w_shapes.append(jax.ShapeDtypeStruct((k, n), BF16))
    if ss is not None:
        in_specs.append(pl.BlockSpec((ss.shape[0], tm, 1), lambda j, i: (0, i, 0)))
        args.append(ss)
    for e, off in extras:
        in_specs.append(pl.BlockSpec((tm, tn), lambda j, i, off=off: (i, j + off)))
        args.append(e)
    out_specs = [pl.BlockSpec((tm, tn), lambda j, i: (i, j))]
    out_shape = [jax.ShapeDtypeStruct((m, n), out_dtype)]
    if next_gain is not None:
        in_specs.append(pl.BlockSpec((1, tn), lambda j, i: (0, j)))
        args.append(next_gain.reshape(1, n))
        out_specs += [pl.BlockSpec((tm, tn), lambda j, i: (i, j)),
                      pl.BlockSpec((None, tm, 1), lambda j, i: (j, i, 0))]
        out_shape += [jax.ShapeDtypeStruct((m, n), BF16), jax.ShapeDtypeStruct((n // tn, m, 1), F32)]
    n_main = len(out_specs)
    if emit_w:
        out_specs += w_specs
        out_shape += w_shapes
    body = functools.partial(_fused_mm_body, n_acts=len(acts), dots=tuple(d._replace(w=None) for d in dots),
                             n_extras=len(extras), epi=epi, has_ss=ss is not None,
                             has_gain=next_gain is not None, emit_w=emit_w)
    res = pl.pallas_call(
        body,
        grid=(n // tn, m // tm),
        in_specs=in_specs,
        out_specs=out_specs,
        out_shape=out_shape,
        scratch_shapes=scratch,
        compiler_params=_params(2),
        name=name,
    )(*args)
    main = res[:n_main] if next_gain is not None else res[0]
    return main, list(res[n_main:])


def _epi_first(accs, extras):
    return accs[0]


def _epi_residual(accs, extras):
    return extras[0] + accs[0]


def _epi_gate_merge(accs, extras):
    return jax.nn.sigmoid(extras[0]) * accs[0] + jax.nn.sigmoid(extras[1]) * accs[1]


def _epi_swiglu(accs, extras):
    return jax.nn.silu(accs[0]) * accs[1]


def _epi_ple(accs, extras):
    return extras[0] + jax.nn.sigmoid(accs[0]) * accs[1]


def _conv_silu_rows(raw_ref, prev_rows, w_ref, b_ref):
    n_rows = raw_ref.shape[0]
    head = raw_ref[:SUBLANES, :]
    ext = jnp.concatenate([prev_rows, head], axis=0)
    acc_head = b_ref[...] + w_ref[CONV_W - 1:CONV_W, :] * head
    acc_rest = b_ref[...] + w_ref[CONV_W - 1:CONV_W, :] * raw_ref[SUBLANES:, :]
    for j in range(1, CONV_W):
        wj = w_ref[CONV_W - 1 - j:CONV_W - j, :]
        acc_head = acc_head + wj * pltpu.roll(ext, j, axis=0)[SUBLANES:, :]
        acc_rest = acc_rest + wj * raw_ref[pl.ds(SUBLANES - j, n_rows - SUBLANES), :]
    acc = jnp.concatenate([acc_head, acc_rest], axis=0)
    return acc * jax.nn.sigmoid(acc)


def _conv_sample_body(x_ref, s_ref, w_ref, b_ref, o_ref):
    steps = x_ref.shape[0]
    xpad = [s_ref[k] for k in range(CONV_W - 1)] + [x_ref[t] for t in range(steps)]
    for t in range(steps):
        acc = b_ref[...] + w_ref[0:1, :] * xpad[t]
        for k in range(1, CONV_W):
            acc = acc + w_ref[k:k + 1, :] * xpad[t + k]
        o_ref[t] = acc * jax.nn.sigmoid(acc)


def _conv_sample(xbc_t, state_t, conv_w, conv_b, layer, tc=1024):
    steps, nb, _ = xbc_t.shape
    return pl.pallas_call(
        _conv_sample_body,
        grid=(CONV_DIM // tc,),
        in_specs=[pl.BlockSpec((steps, nb, tc), lambda c: (0, 0, c)),
                  pl.BlockSpec((None, CONV_W - 1, nb, tc), lambda c: (layer, 0, 0, c)),
                  pl.BlockSpec((None, CONV_W, tc), lambda c: (layer, 0, c)),
                  pl.BlockSpec((1, tc), lambda c: (0, c))],
        out_specs=pl.BlockSpec((steps, nb, tc), lambda c: (0, 0, c)),
        out_shape=jax.ShapeDtypeStruct((steps, nb, CONV_DIM), F32),
        compiler_params=_params(1, 32),
        name="conv_sample",
    )(xbc_t, state_t, conv_w, conv_b.reshape(1, CONV_DIM))


HEADS_PER_DOT = 4
QUAD_COLS = HEADS_PER_DOT * SSD_HEADDIM
CHUNKS_PER_STEP = 2


def _gated_norm(y2, g):
    r = lax.rsqrt(jnp.mean(y2 * y2, axis=-1, keepdims=True) + RMS_EPS)
    return (y2 * r) * g


def _ssd_prompt_body(z_ref, xr_ref, bcr_ref, dtp_ref, cwx_ref, cwb_ref, cbx_ref, cbb_ref,
                     bias_ref, alog_ref, dsk_ref, g_ref, e2_ref,
                     yn_ref, st_ref, s_scr, y_scr, prev_x, prev_bc, *, n_steps):
    c = pl.program_id(1)
    q = SSD_CHUNK
    rows_n = q * CHUNKS_PER_STEP

    @pl.when(c == 0)
    def _():
        s_scr[...] = jnp.zeros_like(s_scr)
        prev_x[...] = jnp.zeros_like(prev_x)
        prev_bc[...] = jnp.zeros_like(prev_bc)

    xs_all = _conv_silu_rows(xr_ref, prev_x[...], cwx_ref, cbx_ref)
    bc_all = _conv_silu_rows(bcr_ref, prev_bc[...], cwb_ref, cbb_ref)
    prev_x[...] = xr_ref[rows_n - SUBLANES:, :]
    prev_bc[...] = bcr_ref[rows_n - SUBLANES:, :]

    lane = lax.broadcasted_iota(jnp.int32, (q, LANES), 1)
    row = lax.broadcasted_iota(jnp.int32, (q, LANES), 0)
    head_ok = lane < SSD_HEADS
    causal = row >= lane
    tri = causal.astype(BF16)
    lane_q = lax.broadcasted_iota(jnp.int32, (q, QUAD_COLS), 1) // SSD_HEADDIM
    head_masks = [(lane_q == k).astype(BF16) for k in range(HEADS_PER_DOT)]
    a = -jnp.exp(alog_ref[...])

    for sub in range(CHUNKS_PER_STEP):
        rs = slice(sub * q, (sub + 1) * q)
        dt = jnp.where(head_ok, _softplus(dtp_ref[rs, :] + bias_ref[...]), 0.0)
        dta = dt * a
        cum3 = jnp.dot(tri, _split(dta, 3), preferred_element_type=F32)
        cum = cum3[:, :LANES] + cum3[:, LANES:2 * LANES] + cum3[:, 2 * LANES:]
        cum_t = cum.T
        dt_t = dt.T
        last = cum[q - 1:q, :]
        ecum = jnp.exp(cum)
        dtw = jnp.where(head_ok, dt * jnp.exp(last - cum), 0.0)
        stack = jnp.concatenate([dtw, ecum], axis=0)
        wide = jnp.dot(_split(stack, 2), e2_ref[...], preferred_element_type=F32)
        dtw_w, ecum_w = wide[:q], wide[q:]
        cdec_w = ecum_w[q - 1:q, :]

        xs = xs_all[rs]
        xs_b = xs.astype(BF16)
        xdd_b = (xs * dtw_w).astype(BF16)
        for g in range(SSD_GROUPS):
            gs = slice(g * GROUP_COLS, (g + 1) * GROUP_COLS)
            b_g = bc_all[rs, g * SSD_STATE:(g + 1) * SSD_STATE]
            c_g = bc_all[rs, (SSD_GROUPS + g) * SSD_STATE:(SSD_GROUPS + g + 1) * SSD_STATE]
            c_b = c_g.astype(BF16)
            cb = lax.dot_general(c_b, b_g.astype(BF16), NT_DIMS, preferred_element_type=F32)
            s_g = s_scr[g]
            y_off = jnp.dot(c_b, s_g.astype(BF16), preferred_element_type=F32)
            y_quads = []
            for quad in range(SSD_HPG // HEADS_PER_DOT):
                h0 = g * SSD_HPG + quad * HEADS_PER_DOT
                m_heads = []
                for k in range(HEADS_PER_DOT):
                    h = h0 + k
                    diff = cum[:, h:h + 1] - cum_t[h:h + 1, :]
                    seg = jnp.exp(jnp.where(causal, diff, -jnp.inf))
                    m_heads.append((cb * seg * dt_t[h:h + 1, :]).astype(BF16))
                xq = xs_b[:, h0 * SSD_HEADDIM:h0 * SSD_HEADDIM + QUAD_COLS]
                rhs = jnp.concatenate([xq * head_masks[k] for k in range(HEADS_PER_DOT)], axis=0)
                y_quads.append(jnp.dot(jnp.concatenate(m_heads, axis=1), rhs, preferred_element_type=F32))
            y_g = jnp.concatenate(y_quads, axis=1) + y_off * ecum_w[:, gs]
            upd = jnp.dot(b_g.T.astype(BF16), xdd_b[:, gs], preferred_element_type=F32)
            s_new = s_g * cdec_w[:, gs] + upd
            s_scr[g] = s_new

            if sub == CHUNKS_PER_STEP - 1:
                @pl.when(c == n_steps - 1)
                def _(s_new=s_new, gs=gs):
                    st_ref[0, gs, :] = s_new.T

            zg = z_ref[rs, gs]
            y_scr[rs, gs] = (y_g + xs[:, gs] * dsk_ref[:, gs]) * (zg * jax.nn.sigmoid(zg))
    yn_ref[...] = _gated_norm(y_scr[...], g_ref[...]).astype(yn_ref.dtype)


def _ssd_prompt(zx, dtp, conv_w, conv_b, layer, bias, alog, dsk_w, g_ssd, e2, nb, seq):
    rows_n = SSD_CHUNK * CHUNKS_PER_STEP
    ns = seq // rows_n
    body = functools.partial(_ssd_prompt_body, n_steps=ns)
    const = lambda b, c: (0, 0)
    rows = lambda b, c: (b * ns + c, 0)
    cb2 = conv_b.reshape(1, CONV_DIM)
    return pl.pallas_call(
        body,
        grid=(nb, ns),
        in_specs=[pl.BlockSpec((rows_n, SSD_INNER), rows),
                  pl.BlockSpec((rows_n, SSD_INNER), lambda b, c: (b * ns + c, OFF_Z // SSD_INNER)),
                  pl.BlockSpec((rows_n, BC_COLS), lambda b, c: (b * ns + c, (OFF_Z + SSD_INNER) // BC_COLS)),
                  pl.BlockSpec((rows_n, LANES), rows),
                  pl.BlockSpec((None, CONV_W, SSD_INNER), lambda b, c: (layer, 0, 0)),
                  pl.BlockSpec((None, CONV_W, BC_COLS), lambda b, c: (layer, 0, SSD_INNER // BC_COLS)),
                  pl.BlockSpec((1, SSD_INNER), const),
                  pl.BlockSpec((1, BC_COLS), lambda b, c: (0, SSD_INNER // BC_COLS)),
                  pl.BlockSpec((1, LANES), const),
                  pl.BlockSpec((1, LANES), const),
                  pl.BlockSpec((1, SSD_INNER), const),
                  pl.BlockSpec((1, SSD_INNER), const),
                  pl.BlockSpec((2 * LANES, SSD_INNER), const)],
        out_specs=[pl.BlockSpec((rows_n, SSD_INNER), rows),
                   pl.BlockSpec((1, SSD_INNER, SSD_STATE), lambda b, c: (b, 0, 0))],
        out_shape=[jax.ShapeDtypeStruct((nb * seq, SSD_INNER), BF16),
                   jax.ShapeDtypeStruct((nb, SSD_INNER, SSD_STATE), F32)],
        scratch_shapes=[pltpu.VMEM((SSD_GROUPS, SSD_STATE, GROUP_COLS), F32),
                        pltpu.VMEM((rows_n, SSD_INNER), F32),
                        pltpu.VMEM((SUBLANES, SSD_INNER), F32),
                        pltpu.VMEM((SUBLANES, BC_COLS), F32)],
        compiler_params=_params(2, 56),
        name="ssd_prompt",
    )(zx, zx, zx, dtp, conv_w, conv_w, cb2, cb2, bias, alog, dsk_w, g_ssd, e2)


SEQ_PER_STEP = 4


def _ssd_sample_body(xc_ref, z_ref, dtp_ref, h0_ref, bias_ref, alog_ref, dsk_ref, g_ref, e3_ref, *rest, steps):
    yn_ref, hn_ref = rest[-2:]
    rows_n = SEQ_PER_STEP * steps
    lane = lax.broadcasted_iota(jnp.int32, (rows_n, LANES), 1)
    row = lax.broadcasted_iota(jnp.int32, (rows_n, LANES), 0)
    tpos = row % steps
    head_ok = lane < SSD_HEADS

    def shift(v, j, tp):
        return jnp.where(tp >= j, pltpu.roll(v, j, axis=0), 0.0)

    dt = jnp.where(head_ok, _softplus(dtp_ref[...] + bias_ref[...]), 0.0)
    a = -jnp.exp(alog_ref[...])
    dta = dt * a
    cum = dta
    for j in range(1, steps):
        cum = cum + shift(dta, j, tpos)
    last = jnp.zeros_like(cum)
    for k in range(steps):
        back = steps - 1 - k
        src = cum if back == 0 else pltpu.roll(cum, rows_n - back, axis=0)
        last = jnp.where(tpos == k, src, last)
    ecum = jnp.exp(cum)
    dte = jnp.where(head_ok, jnp.exp(last - cum), 0.0)
    cdec = jnp.exp(last)

    b_all = xc_ref[:, SSD_INNER:SSD_INNER + SSD_GROUPS * SSD_STATE]
    c_all = xc_ref[:, SSD_INNER + SSD_GROUPS * SSD_STATE:]
    row_w = lax.broadcasted_iota(jnp.int32, b_all.shape, 0)
    tpos_w = row_w % steps
    coefs = []
    for j in range(steps):
        prod = c_all * (b_all if j == 0 else shift(b_all, j, tpos_w))
        cb = jnp.zeros((rows_n, LANES), F32)
        for g in range(SSD_GROUPS):
            cbg = jnp.sum(prod[:, g * SSD_STATE:(g + 1) * SSD_STATE], axis=1, keepdims=True)
            cb = jnp.where((lane // SSD_HPG) == g, cbg, cb)
        if j == 0:
            coefs.append(jnp.where(head_ok, cb, 0.0))
        else:
            seg = jnp.exp(cum - shift(cum, j, tpos))
            coefs.append(jnp.where(head_ok & (tpos >= j), cb * seg, 0.0))
    stack = jnp.concatenate([dt, dte, ecum, cdec] + coefs, axis=0)
    wide = jnp.dot(_split(stack, 3), e3_ref[...], preferred_element_type=F32)
    dt_w, dte_w = wide[:rows_n], wide[rows_n:2 * rows_n]
    ecum_w, cdec_w = wide[2 * rows_n:3 * rows_n], wide[3 * rows_n:4 * rows_n]

    xs = xc_ref[:, :SSD_INNER]
    xdt = xs * dt_w
    y = wide[4 * rows_n:5 * rows_n] * xdt
    for j in range(1, steps):
        y = y + wide[(4 + j) * rows_n:(5 + j) * rows_n] * pltpu.roll(xdt, j, axis=0)
    xdd = xdt * dte_w

    row_g = lax.broadcasted_iota(jnp.int32, (rows_n, GROUP_COLS), 0)
    ones_b = jnp.ones((rows_n, SSD_STATE), BF16)
    c_bf = c_all.astype(BF16)
    b_bf = b_all.astype(BF16)
    y_off_groups = []
    for g in range(SSD_GROUPS):
        gs = slice(g * GROUP_COLS, (g + 1) * GROUP_COLS)
        ns = slice(g * SSD_STATE, (g + 1) * SSD_STATE)
        y_off = jnp.zeros((rows_n, GROUP_COLS), F32)
        for i in range(SEQ_PER_STEP):
            mine = (row_g // steps) == i
            h0g = h0_ref[i, gs, :]
            yo = lax.dot_general(c_bf[:, ns], h0g.astype(BF16), NT_DIMS, preferred_element_type=F32)
            y_off = jnp.where(mine, yo, y_off)
            x_i = jnp.where(mine, xdd[:, gs], 0.0).astype(BF16)
            upd = lax.dot_general(x_i, b_bf[:, ns], (((0,), (0,)), ((), ())),
                                  preferred_element_type=F32)
            cd = cdec_w[i * steps:i * steps + 1, gs]
            cd_hi = cd.astype(BF16).astype(F32)
            cd_mid = (cd - cd_hi).astype(BF16).astype(F32)
            cd_lo = cd - cd_hi - cd_mid
            cd_rows = jnp.where(row_g == 0, cd_hi,
                                jnp.where(row_g == 1, cd_mid,
                                          jnp.where(row_g == 2, cd_lo, 0.0))).astype(BF16)
            cd_col = lax.dot_general(cd_rows, ones_b, (((0,), (0,)), ((), ())),
                                     preferred_element_type=F32)
            hn_ref[i, gs, :] = h0g * cd_col + upd
        y_off_groups.append(y_off)
    y = y + jnp.concatenate(y_off_groups, axis=1) * ecum_w
    zz = z_ref[...]
    y2 = (y + xs * dsk_ref[...]) * (zz * jax.nn.sigmoid(zz))
    yn_ref[...] = _gated_norm(y2, g_ref[...]).astype(yn_ref.dtype)


def _ssd_sample(xc, z, dtp, h_all, layer, new_states, bias, alog, dsk_w, g_ssd, e3, steps):
    nb = h_all.shape[1]
    rows_n = SEQ_PER_STEP * steps
    body = functools.partial(_ssd_sample_body, steps=steps)
    const = lambda i: (0, 0)
    state_spec = pl.BlockSpec((None, SEQ_PER_STEP, SSD_INNER, SSD_STATE), lambda i: (layer, i, 0, 0))
    in_specs = [pl.BlockSpec((rows_n, CONV_DIM), lambda i: (i, 0)),
                pl.BlockSpec((rows_n, SSD_INNER), lambda i: (i, 0)),
                pl.BlockSpec((rows_n, LANES), lambda i: (i, 0)),
                state_spec,
                pl.BlockSpec((1, LANES), const),
                pl.BlockSpec((1, LANES), const),
                pl.BlockSpec((1, SSD_INNER), const),
                pl.BlockSpec((1, SSD_INNER), const),
                pl.BlockSpec((3 * LANES, SSD_INNER), const)]
    args = [xc, z, dtp, h_all, bias, alog, dsk_w, g_ssd, e3]
    aliases = {}
    if new_states is not None:
        in_specs.append(pl.BlockSpec(memory_space=pl.ANY))
        args.append(new_states)
        aliases = {len(args) - 1: 1}
    return pl.pallas_call(
        body,
        grid=(nb // SEQ_PER_STEP,),
        in_specs=in_specs,
        out_specs=[pl.BlockSpec((rows_n, SSD_INNER), lambda i: (i, 0)), state_spec],
        out_shape=[jax.ShapeDtypeStruct((nb * steps, SSD_INNER), BF16),
                   jax.ShapeDtypeStruct(h_all.shape, F32)],
        input_output_aliases=aliases,
        compiler_params=_params(1, 56),
        name="ssd_sample",
    )(*args)


def _vnorm(v, g):
    v = jax.nn.gelu(v)
    r = lax.rsqrt(jnp.mean(v * v, axis=-1, keepdims=True) + RMS_EPS)
    return (v * r) * g


def _cmlp_prompt_body(u_ref, v_ref, ws_ref, bst_ref, gv_ref, cm_ref, vs_ref, *, n_chunks):
    q = CMLP_CHUNK
    vn = _vnorm(v_ref[...], gv_ref[...])

    @pl.when(pl.program_id(0) % n_chunks == n_chunks - 1)
    def _():
        vs_ref[0] = vn

    u = jax.nn.gelu(u_ref[...])
    row = lax.broadcasted_iota(jnp.int32, (q, q), 0)
    col = lax.broadcasted_iota(jnp.int32, (q, q), 1)
    vb = vn.astype(BF16)
    for g in range(CMLP_GROUPS):
        gs = slice(g * CMLP_GROUP_DIM, (g + 1) * CMLP_GROUP_DIM)
        wm = jnp.where(row >= col, ws_ref[g], 0.0).astype(BF16)
        mixed = jnp.dot(wm, vb[:, gs], preferred_element_type=F32) + bst_ref[:, g:g + 1]
        cm_ref[:, gs] = (u[:, gs] * mixed).astype(cm_ref.dtype)


def _cmlp_prompt(uvg, w_s, b_s, g_v, layer, nb, seq):
    q = CMLP_CHUNK
    nc = seq // q
    body = functools.partial(_cmlp_prompt_body, n_chunks=nc)
    return pl.pallas_call(
        body,
        grid=(nb * nc,),
        in_specs=[pl.BlockSpec((q, CMLP_WIDTH), lambda i: (i, 0)),
                  pl.BlockSpec((q, CMLP_WIDTH), lambda i: (i, 1)),
                  pl.BlockSpec((None, CMLP_GROUPS, q, q), lambda i: (layer, 0, 0, 0)),
                  pl.BlockSpec((q, CMLP_GROUPS), lambda i: (0, 0)),
                  pl.BlockSpec((1, CMLP_WIDTH), lambda i: (0, 0))],
        out_specs=[pl.BlockSpec((q, CMLP_WIDTH), lambda i: (i, 0)),
                   pl.BlockSpec((1, q, CMLP_WIDTH), lambda i: (i // nc, 0, 0))],
        out_shape=[jax.ShapeDtypeStruct((nb * seq, CMLP_WIDTH), BF16),
                   jax.ShapeDtypeStruct((nb, q, CMLP_WIDTH), F32)],
        compiler_params=_params(1, 32),
        name="cmlp_prompt",
    )(uvg, uvg, w_s, b_s.T, g_v.reshape(1, CMLP_WIDTH))


def _cmlp_sample_body(u_ref, v_ref, wrow_ref, brow_ref, gv_ref, cm_ref, vs_ref):
    steps = u_ref.shape[0]
    vn = [_vnorm(v_ref[t], gv_ref[...]) for t in range(steps)]
    for t in range(steps):
        vs_ref[t] = vn[t]
        mixed = brow_ref[t:t + 1, :] + wrow_ref[t, 0:1, :] * vn[0]
        for s in range(1, t + 1):
            mixed = mixed + wrow_ref[t, s:s + 1, :] * vn[s]
        cm_ref[t] = (jax.nn.gelu(u_ref[t]) * mixed).astype(cm_ref.dtype)


def _cmlp_sample(uvg_t, w_s, b_s, g_v, tb=32):
    steps, nb, _ = uvg_t.shape
    wrow = jnp.repeat(jnp.transpose(w_s[:, :steps, :steps], (1, 2, 0)), CMLP_GROUP_DIM, axis=-1)
    brow = jnp.repeat(b_s[:, :steps].T, CMLP_GROUP_DIM, axis=-1)
    return pl.pallas_call(
        _cmlp_sample_body,
        grid=(nb // tb,),
        in_specs=[pl.BlockSpec((steps, tb, CMLP_WIDTH), lambda i: (0, i, 0)),
                  pl.BlockSpec((steps, tb, CMLP_WIDTH), lambda i: (0, i, 1)),
                  pl.BlockSpec((steps, steps, CMLP_WIDTH), lambda i: (0, 0, 0)),
                  pl.BlockSpec((steps, CMLP_WIDTH), lambda i: (0, 0)),
                  pl.BlockSpec((1, CMLP_WIDTH), lambda i: (0, 0))],
        out_specs=[pl.BlockSpec((steps, tb, CMLP_WIDTH), lambda i: (0, i, 0)),
                   pl.BlockSpec((steps, tb, CMLP_WIDTH), lambda i: (0, i, 0))],
        out_shape=[jax.ShapeDtypeStruct((steps, nb, CMLP_WIDTH), BF16),
                   jax.ShapeDtypeStruct((steps, nb, CMLP_WIDTH), F32)],
        compiler_params=_params(1, 32),
        name="cmlp_sample",
    )(uvg_t, uvg_t, wrow, brow, g_v.reshape(1, CMLP_WIDTH))


def _layer(x, hg, ss, ple_all, w, gains, layer, mixers, cast_w):
    first = cast_w is None
    tm = 1024 if first else 512
    new_cast = {}

    def mm(name, acts, dots, extras, epi, share_w=False, **kw):
        if share_w and not first:
            dots = [d._replace(w=cw, off=0, shifted=False) for d, cw in zip(dots, cast_w[name])]
        res, new_cast[name] = _fused_mm(acts, dots, extras, epi, layer=layer, name=name,
                                        emit_w=share_w and first, **kw)
        return res

    w_in_t = w["w_in_t"]
    zx = mm("in_proj_zx", [hg], [Dot(0, w_in_t, 0, scaled=True, transposed=True)], [], _epi_first, ss=ss,
            share_w=True, n=OFF_XBC, tm=tm, tn=1024, out_dtype=F32)
    dtp = mm("in_proj_dt", [hg], [Dot(0, w_in_t, OFF_XBC // LANES, scaled=True, transposed=True)], [],
             _epi_first, ss=ss, share_w=True, n=LANES, tm=tm, tn=LANES, out_dtype=F32)
    uvg = mm("in_proj_uvg", [hg], [Dot(0, w_in_t, OFF_XBC // 1024, scaled=True, transposed=True, shifted=True)],
             [], _epi_first, ss=ss, share_w=True, n=4 * D_MODEL, tm=tm, tn=1024, out_dtype=F32)
    yn, cm, states = mixers(zx, dtp, uvg)
    tn = 512
    merged = mm("branch_merge", [yn, cm], [Dot(0, w["w_br_a"]), Dot(1, w["w_br_b"])],
                [(uvg, 2 * D_MODEL // tn), (uvg, 3 * D_MODEL // tn)], _epi_gate_merge,
                n=D_MODEL, tm=512, tn=tn, out_dtype=BF16)
    x, hn, ss = mm("out_proj", [merged], [Dot(0, w["w_out"])], [(x, 0)], _epi_residual,
                   next_gain=gains["g_ffn"], n=D_MODEL, tm=tm, tn=1024, out_dtype=F32)
    act = mm("ffn_up", [hn],
             [Dot(0, w["w_gate_up"], 0, scaled=True), Dot(0, w["w_gate_up"], FFN_HIDDEN // tn, scaled=True)],
             [], _epi_swiglu, ss=ss, share_w=True, n=FFN_HIDDEN, tm=tm, tn=tn, out_dtype=BF16)
    x, hp, ss = mm("ffn_down", [act], [Dot(0, w["w_down"])], [(x, 0)], _epi_residual,
                   next_gain=gains["g_ple"], n=D_MODEL, tm=512, tn=tn, out_dtype=F32)
    ple_dots = [Dot(0, w["w_ple_gate"], 0, scaled=True), Dot(1, w["w_ple"])]
    if gains["g_next"] is None:
        x = mm("ple", [hp, ple_all], ple_dots, [(x, 0)], _epi_ple, ss=ss,
               n=D_MODEL, tm=512, tn=1024, out_dtype=F32)
        return x, None, None, states, new_cast
    x, hg, ss = mm("ple", [hp, ple_all], ple_dots, [(x, 0)], _epi_ple, ss=ss, next_gain=gains["g_next"],
                   n=D_MODEL, tm=512, tn=1024, out_dtype=F32)
    return x, hg, ss, states, new_cast


def kernel(x_prompt, x_sample, p_prompt, p_sample, state_ssd, state_conv, g_mix, w_in, conv_w, conv_b,
           dt_bias, a_log, d_skip, g_ssd, w_br_a, g_v, w_s, b_s, w_br_b, w_out, g_ffn, w_gate_up, w_down,
           g_ple, w_ple_gate, w_ple, g_final):
    nbp, seq, _ = x_prompt.shape
    nbs, steps, _ = x_sample.shape
    assert seq % (SSD_CHUNK * CHUNKS_PER_STEP) == 0 and seq % CMLP_CHUNK == 0 and steps <= SSD_CHUNK

    head_rows = lax.broadcasted_iota(jnp.int32, (LANES, SSD_INNER), 0)
    head_cols = lax.broadcasted_iota(jnp.int32, (LANES, SSD_INNER), 1) // SSD_HEADDIM
    expand = (head_rows == head_cols).astype(BF16)
    e2, e3 = jnp.tile(expand, (2, 1)), jnp.tile(expand, (3, 1))
    pad_heads = lambda v: jnp.pad(v, (0, LANES - SSD_HEADS)).reshape(1, LANES)
    w = dict(w_in_t=jnp.transpose(w_in, (0, 2, 1)), w_br_a=w_br_a, w_br_b=w_br_b, w_out=w_out,
             w_gate_up=w_gate_up, w_down=w_down, w_ple_gate=w_ple_gate, w_ple=w_ple)

    xp = x_prompt.reshape(nbp * seq, D_MODEL)
    xs = jnp.transpose(x_sample, (1, 0, 2)).reshape(steps * nbs, D_MODEL)
    ple_p = p_prompt.reshape(DEPTH, nbp * seq, PLE_DIM)
    ple_s = jnp.transpose(p_sample, (0, 2, 1, 3)).reshape(DEPTH, steps * nbs, PLE_DIM)
    conv_state_t = jnp.transpose(state_conv, (0, 2, 1, 3))
    h_all = state_ssd.reshape(DEPTH, nbs, SSD_INNER, SSD_STATE)
    new_states = None
    hgp, ssp = _prenorm(xp, g_mix[0])
    hgs, sss = _prenorm(xs, g_mix[0])
    outs = {k: [] for k in ("ssd_p", "conv_p", "v_p", "conv_s", "v_s")}
    for i in range(DEPTH):
        gains = dict(g_ffn=g_ffn[i], g_ple=g_ple[i], g_next=g_mix[i + 1] if i + 1 < DEPTH else None)
        bias, alog = pad_heads(dt_bias[i]), pad_heads(a_log[i])
        dsk_w = jnp.repeat(d_skip[i], SSD_HEADDIM).reshape(1, SSD_INNER)
        gs = g_ssd[i].reshape(1, SSD_INNER)

        def prompt_mixers(zx, dtp, uvg, i=i, bias=bias, alog=alog, dsk_w=dsk_w, gs=gs):
            yn, st = _ssd_prompt(zx, dtp, conv_w, conv_b[i], i, bias, alog, dsk_w, gs, e2, nbp, seq)
            cm, vst = _cmlp_prompt(uvg, w_s, b_s[i], g_v[i], i, nbp, seq)
            conv_new = zx.reshape(nbp, seq, OFF_XBC)[:, seq - (CONV_W - 1):, OFF_Z:]
            return yn, cm, (st.reshape(nbp, SSD_HEADS, SSD_HEADDIM, SSD_STATE), conv_new, vst)

        def sample_mixers(zx, dtp, uvg, i=i, bias=bias, alog=alog, dsk_w=dsk_w, gs=gs, new_states=new_states):
            zx_t = zx.reshape(steps, nbs, OFF_XBC)
            xbc_t = zx_t[:, :, OFF_Z:]
            xc_t = _conv_sample(xbc_t, conv_state_t, conv_w, conv_b[i], i)
            seq_major = lambda t, wd: jnp.transpose(t.reshape(steps, nbs, wd), (1, 0, 2)).reshape(nbs * steps, wd)
            yn_b, st = _ssd_sample(seq_major(xc_t, CONV_DIM), seq_major(zx_t[:, :, :OFF_Z], OFF_Z),
                                   seq_major(dtp, LANES), h_all, i, new_states,
                                   bias, alog, dsk_w, gs, e3, steps)
            yn = jnp.transpose(yn_b.reshape(nbs, steps, SSD_INNER), (1, 0, 2)).reshape(steps * nbs, SSD_INNER)
            cm_t, vn_t = _cmlp_sample(uvg.reshape(steps, nbs, 4 * D_MODEL), w_s[i], b_s[i], g_v[i])
            conv_new = jnp.transpose(xbc_t[steps - (CONV_W - 1):], (1, 0, 2))
            return (yn, cm_t.reshape(steps * nbs, CMLP_WIDTH),
                    (st, conv_new, jnp.transpose(vn_t, (1, 0, 2))))

        xp, hgp, ssp, (hp_, cp_, vp_), cast_w = _layer(xp, hgp, ssp, ple_p, w, gains, i, prompt_mixers, None)
        xs, hgs, sss, (new_states, cs_, vs_), _ = _layer(xs, hgs, sss, ple_s, w, gains, i, sample_mixers, cast_w)
        outs["ssd_p"].append(hp_); outs["conv_p"].append(cp_); outs["v_p"].append(vp_)
        outs["conv_s"].append(cs_); outs["v_s"].append(vs_)

    y_prompt = _rownorm(xp, g_final, F32).reshape(nbp, seq, D_MODEL)
    y_sample = jnp.transpose(_rownorm(xs, g_final, F32).reshape(steps, nbs, D_MODEL), (1, 0, 2))
    ssd_s = new_states.reshape(DEPTH, nbs, SSD_HEADS, SSD_HEADDIM, SSD_STATE)
    return (y_prompt, y_sample, jnp.stack(outs["ssd_p"]), jnp.stack(outs["conv_p"]), jnp.stack(outs["v_p"]),
            ssd_s, jnp.stack(outs["conv_s"]), jnp.stack(outs["v_s"]))
```

```python
import functools
from typing import Any, NamedTuple

import jax
import jax.numpy as jnp
from jax import lax
from jax.experimental import pallas as pl
from jax.experimental.pallas import tpu as pltpu

F32 = jnp.float32
BF16 = jnp.bfloat16

D_MODEL = 2048
DEPTH = 4
PLE_DIM = 256
RMS_EPS = 1e-6
SSD_INNER = 4096
SSD_HEADDIM = 64
SSD_HEADS = 64
SSD_GROUPS = 8
SSD_HPG = 8
SSD_STATE = 128
SSD_CHUNK = 128
CONV_W = 4
CONV_DIM = 6144
BC_COLS = 2 * SSD_GROUPS * SSD_STATE
CMLP_WIDTH = 2048
CMLP_GROUPS = 16
CMLP_GROUP_DIM = 128
CMLP_CHUNK = 128
FFN_HIDDEN = 5632
OFF_Z = SSD_INNER
OFF_XBC = OFF_Z + CONV_DIM
OFF_DT = OFF_XBC + SSD_HEADS
IN_WIDTH = OFF_DT + 4 * D_MODEL
GROUP_COLS = SSD_HPG * SSD_HEADDIM
LANES = 128
SUBLANES = 8
HALF_LANES = LANES // 2
VMEM_LIMIT_MB = 58
NT_DIMS = (((1,), (1,)), ((), ()))


def _params(n_axes, vmem_mb=VMEM_LIMIT_MB):
    return pltpu.CompilerParams(dimension_semantics=("arbitrary",) * n_axes,
                                vmem_limit_bytes=vmem_mb * 2 ** 20)


def _softplus(x):
    return jnp.maximum(x, 0.0) + jnp.log1p(jnp.exp(-jnp.abs(x)))


def _split(q, parts):
    out, rem = [], q
    for _ in range(parts):
        piece = rem.astype(BF16)
        out.append(piece)
        rem = rem - piece.astype(F32)
    return jnp.concatenate(out, axis=1)


def _inv_rms(ss_parts, width):
    return lax.rsqrt(jnp.sum(ss_parts, axis=0) / width + RMS_EPS)


def _rownorm_body(x_ref, g_ref, o_ref):
    xf = x_ref[...]
    r = lax.rsqrt(jnp.mean(xf * xf, axis=-1, keepdims=True) + RMS_EPS)
    o_ref[...] = ((xf * r) * g_ref[...]).astype(o_ref.dtype)


def _rownorm(x, g, out_dtype, tr=512):
    m, d = x.shape
    return pl.pallas_call(
        _rownorm_body,
        grid=(m // tr,),
        in_specs=[pl.BlockSpec((tr, d), lambda i: (i, 0)),
                  pl.BlockSpec((1, d), lambda i: (0, 0))],
        out_specs=pl.BlockSpec((tr, d), lambda i: (i, 0)),
        out_shape=jax.ShapeDtypeStruct((m, d), out_dtype),
        compiler_params=_params(1, 32),
        name="rownorm",
    )(x, g.reshape(1, d))


def _prenorm_body(x_ref, g_ref, xg_ref, ss_ref):
    xf = x_ref[...]
    xg_ref[...] = (xf * g_ref[...]).astype(xg_ref.dtype)
    ss_ref[...] = jnp.sum(xf * xf, axis=-1, keepdims=True)


def _prenorm(x, g, tr=512):
    m, d = x.shape
    return pl.pallas_call(
        _prenorm_body,
        grid=(m // tr,),
        in_specs=[pl.BlockSpec((tr, d), lambda i: (i, 0)),
                  pl.BlockSpec((1, d), lambda i: (0, 0))],
        out_specs=[pl.BlockSpec((tr, d), lambda i: (i, 0)),
                   pl.BlockSpec((None, tr, 1), lambda i: (0, i, 0))],
        out_shape=[jax.ShapeDtypeStruct((m, d), BF16), jax.ShapeDtypeStruct((1, m, 1), F32)],
        compiler_params=_params(1, 32),
        name="prenorm",
    )(x, g.reshape(1, d))


class Dot(NamedTuple):
    act: int
    w: Any
    off: int = 0
    scaled: bool = False
    transposed: bool = False
    shifted: bool = False


def _fused_mm_body(*refs, n_acts, dots, n_extras, epi, has_ss, has_gain):
    a_refs = refs[:n_acts]
    pos = n_acts
    w_refs = []
    for d in dots:
        w_refs.append(refs[pos:pos + (2 if d.shifted else 1)])
        pos += 2 if d.shifted else 1
    ss_ref = refs[pos] if has_ss else None
    pos += has_ss
    e_refs = refs[pos:pos + n_extras]
    pos += n_extras
    gain_ref = refs[pos] if has_gain else None
    pos += has_gain
    o_ref = refs[pos]
    pos += 1
    if has_gain:
        xg_ref, ssq_ref = refs[pos:pos + 2]
        pos += 2
    scr = refs[pos:]

    @pl.when(pl.program_id(1) == 0)
    def _():
        for d, wr, s in zip(dots, w_refs, scr):
            if d.shifted:
                keep = s.shape[0] - HALF_LANES
                s[:keep, :] = wr[0][HALF_LANES:, :].astype(BF16)
                s[keep:, :] = wr[1][...].astype(BF16)
            else:
                s[...] = wr[0][...].astype(BF16)

    acts = [a[...].astype(BF16) for a in a_refs]
    r = _inv_rms(ss_ref[...], D_MODEL) if has_ss else None
    accs = []
    for d, s in zip(dots, scr):
        if d.transposed:
            acc = lax.dot_general(acts[d.act], s[...], NT_DIMS, preferred_element_type=F32)
        else:
            acc = jnp.dot(acts[d.act], s[...], preferred_element_type=F32)
        accs.append(acc * r if d.scaled else acc)
    out = epi(accs, [e[...] for e in e_refs])
    o_ref[...] = out.astype(o_ref.dtype)
    if has_gain:
        xg_ref[...] = (out * gain_ref[...]).astype(xg_ref.dtype)
        ssq_ref[...] = jnp.sum(out * out, axis=-1, keepdims=True)


def _fused_mm(acts, dots, extras, epi, *, layer, n, tm, tn, out_dtype, name, ss=None, next_gain=None):
    m = acts[0].shape[-2]
    tm = min(tm, m)
    in_specs, args, scratch = [], [], []
    for a in acts:
        if a.ndim == 3:
            in_specs.append(pl.BlockSpec((None, tm, a.shape[2]), lambda j, i: (layer, i, 0)))
        else:
            in_specs.append(pl.BlockSpec((tm, a.shape[1]), lambda j, i: (i, 0)))
        args.append(a)
    for d in dots:
        k = acts[d.act].shape[-1]
        if d.transposed:
            in_specs.append(pl.BlockSpec((None, tn, k), lambda j, i, off=d.off: (layer, j + off, 0)))
            args.append(d.w)
            if d.shifted:
                per = tn // HALF_LANES
                in_specs.append(pl.BlockSpec((None, HALF_LANES, k),
                                             lambda j, i, off=d.off, per=per: (layer, (j + off + 1) * per, 0)))
                args.append(d.w)
            scratch.append(pltpu.VMEM((tn, k), BF16))
        else:
            in_specs.append(pl.BlockSpec((None, k, tn), lambda j, i, off=d.off: (layer, 0, j + off)))
            args.append(d.w)
            scratch.append(pltpu.VMEM((k, tn), BF16))
    if ss is not None:
        in_specs.append(pl.BlockSpec((ss.shape[0], tm, 1), lambda j, i: (0, i, 0)))
        args.append(ss)
    for e, off in extras:
        in_specs.append(pl.BlockSpec((tm, tn), lambda j, i, off=off: (i, j + off)))
        args.append(e)
    out_specs = [pl.BlockSpec((tm, tn), lambda j, i: (i, j))]
    out_shape = [jax.ShapeDtypeStruct((m, n), out_dtype)]
    if next_gain is not None:
        in_specs.append(pl.BlockSpec((1, tn), lambda j, i: (0, j)))
        args.append(next_gain.reshape(1, n))
        out_specs += [pl.BlockSpec((tm, tn), lambda j, i: (i, j)),
                      pl.BlockSpec((None, tm, 1), lambda j, i: (j, i, 0))]
        out_shape += [jax.ShapeDtypeStruct((m, n), BF16), jax.ShapeDtypeStruct((n // tn, m, 1), F32)]
    body = functools.partial(_fused_mm_body, n_acts=len(acts), dots=tuple(d._replace(w=None) for d in dots),
                             n_extras=len(extras), epi=epi, has_ss=ss is not None,
                             has_gain=next_gain is not None)
    res = pl.pallas_call(
        body,
        grid=(n // tn, m // tm),
        in_specs=in_specs,
        out_specs=out_specs,
        out_shape=out_shape,
        scratch_shapes=scratch,
        compiler_params=_params(2),
        name=name,
    )(*args)
    return res if next_gain is not None else res[0]


def _epi_first(accs, extras):
    return accs[0]


def _epi_residual(accs, extras):
    return extras[0] + accs[0]


def _epi_gate_merge(accs, extras):
    return (jax.nn.sigmoid(extras[0].astype(F32)) * accs[0]
            + jax.nn.sigmoid(extras[1].astype(F32)) * accs[1])


def _epi_swiglu(accs, extras):
    return jax.nn.silu(accs[0]) * accs[1]


def _epi_ple(accs, extras):
    return extras[0] + jax.nn.sigmoid(accs[0]) * accs[1]


def _conv_silu_rows(raw_ref, prev_rows, w_ref, b_ref):
    n_rows = raw_ref.shape[0]
    head = raw_ref[:SUBLANES, :]
    ext = jnp.concatenate([prev_rows, head], axis=0)
    acc_head = b_ref[...] + w_ref[CONV_W - 1:CONV_W, :] * head
    acc_rest = b_ref[...] + w_ref[CONV_W - 1:CONV_W, :] * raw_ref[SUBLANES:, :]
    for j in range(1, CONV_W):
        wj = w_ref[CONV_W - 1 - j:CONV_W - j, :]
        acc_head = acc_head + wj * pltpu.roll(ext, j, axis=0)[SUBLANES:, :]
        acc_rest = acc_rest + wj * raw_ref[pl.ds(SUBLANES - j, n_rows - SUBLANES), :]
    acc = jnp.concatenate([acc_head, acc_rest], axis=0)
    return acc * jax.nn.sigmoid(acc)


def _conv_sample_body(x_ref, s_ref, w_ref, b_ref, o_ref):
    steps = x_ref.shape[0]
    xpad = [s_ref[k] for k in range(CONV_W - 1)] + [x_ref[t] for t in range(steps)]
    for t in range(steps):
        acc = b_ref[...] + w_ref[0:1, :] * xpad[t]
        for k in range(1, CONV_W):
            acc = acc + w_ref[k:k + 1, :] * xpad[t + k]
        o_ref[t] = acc * jax.nn.sigmoid(acc)


def _conv_sample(xbc_t, state_t, conv_w, conv_b, layer, tc=1024):
    steps, nb, _ = xbc_t.shape
    return pl.pallas_call(
        _conv_sample_body,
        grid=(CONV_DIM // tc,),
        in_specs=[pl.BlockSpec((steps, nb, tc), lambda c: (0, 0, c)),
                  pl.BlockSpec((None, CONV_W - 1, nb, tc), lambda c: (layer, 0, 0, c)),
                  pl.BlockSpec((None, CONV_W, tc), lambda c: (layer, 0, c)),
                  pl.BlockSpec((1, tc), lambda c: (0, c))],
        out_specs=pl.BlockSpec((steps, nb, tc), lambda c: (0, 0, c)),
        out_shape=jax.ShapeDtypeStruct((steps, nb, CONV_DIM), F32),
        compiler_params=_params(1, 32),
        name="conv_sample",
    )(xbc_t, state_t, conv_w, conv_b.reshape(1, CONV_DIM))


HEADS_PER_DOT = 4
QUAD_COLS = HEADS_PER_DOT * SSD_HEADDIM
CHUNKS_PER_STEP = 2


def _gated_norm(y2, g):
    r = lax.rsqrt(jnp.mean(y2 * y2, axis=-1, keepdims=True) + RMS_EPS)
    return (y2 * r) * g


def _ssd_prompt_body(z_ref, xr_ref, bcr_ref, dtp_ref, cwx_ref, cwb_ref, cbx_ref, cbb_ref,
                     bias_ref, alog_ref, dsk_ref, g_ref, e2_ref,
                     yn_ref, st_ref, s_scr, y_scr, prev_x, prev_bc, *, n_steps):
    c = pl.program_id(1)
    q = SSD_CHUNK
    rows_n = q * CHUNKS_PER_STEP

    @pl.when(c == 0)
    def _():
        s_scr[...] = jnp.zeros_like(s_scr)
        prev_x[...] = jnp.zeros_like(prev_x)
        prev_bc[...] = jnp.zeros_like(prev_bc)

    xs_all = _conv_silu_rows(xr_ref, prev_x[...], cwx_ref, cbx_ref)
    bc_all = _conv_silu_rows(bcr_ref, prev_bc[...], cwb_ref, cbb_ref)
    prev_x[...] = xr_ref[rows_n - SUBLANES:, :]
    prev_bc[...] = bcr_ref[rows_n - SUBLANES:, :]

    lane = lax.broadcasted_iota(jnp.int32, (q, LANES), 1)
    row = lax.broadcasted_iota(jnp.int32, (q, LANES), 0)
    head_ok = lane < SSD_HEADS
    causal = row >= lane
    tri = causal.astype(BF16)
    lane_q = lax.broadcasted_iota(jnp.int32, (q, QUAD_COLS), 1) // SSD_HEADDIM
    head_masks = [(lane_q == k).astype(BF16) for k in range(HEADS_PER_DOT)]
    a = -jnp.exp(alog_ref[...])

    for sub in range(CHUNKS_PER_STEP):
        rs = slice(sub * q, (sub + 1) * q)
        dt = jnp.where(head_ok, _softplus(dtp_ref[rs, :] + bias_ref[...]), 0.0)
        dta = dt * a
        cum3 = jnp.dot(tri, _split(dta, 3), preferred_element_type=F32)
        cum = cum3[:, :LANES] + cum3[:, LANES:2 * LANES] + cum3[:, 2 * LANES:]
        cum_t = cum.T
        dt_t = dt.T
        last = cum[q - 1:q, :]
        ecum = jnp.exp(cum)
        dtw = jnp.where(head_ok, dt * jnp.exp(last - cum), 0.0)
        stack = jnp.concatenate([dtw, ecum], axis=0)
        wide = jnp.dot(_split(stack, 2), e2_ref[...], preferred_element_type=F32)
        dtw_w, ecum_w = wide[:q], wide[q:]
        cdec_w = ecum_w[q - 1:q, :]

        xs = xs_all[rs]
        xs_b = xs.astype(BF16)
        xdd_b = (xs * dtw_w).astype(BF16)
        for g in range(SSD_GROUPS):
            gs = slice(g * GROUP_COLS, (g + 1) * GROUP_COLS)
            b_g = bc_all[rs, g * SSD_STATE:(g + 1) * SSD_STATE]
            c_g = bc_all[rs, (SSD_GROUPS + g) * SSD_STATE:(SSD_GROUPS + g + 1) * SSD_STATE]
            c_b = c_g.astype(BF16)
            cb = lax.dot_general(c_b, b_g.astype(BF16), NT_DIMS, preferred_element_type=F32)
            s_g = s_scr[g]
            y_off = jnp.dot(c_b, s_g.astype(BF16), preferred_element_type=F32)
            y_quads = []
            for quad in range(SSD_HPG // HEADS_PER_DOT):
                h0 = g * SSD_HPG + quad * HEADS_PER_DOT
                m_heads = []
                for k in range(HEADS_PER_DOT):
                    h = h0 + k
                    diff = cum[:, h:h + 1] - cum_t[h:h + 1, :]
                    seg = jnp.exp(jnp.where(causal, diff, -jnp.inf))
                    m_heads.append((cb * seg * dt_t[h:h + 1, :]).astype(BF16))
                xq = xs_b[:, h0 * SSD_HEADDIM:h0 * SSD_HEADDIM + QUAD_COLS]
                rhs = jnp.concatenate([xq * head_masks[k] for k in range(HEADS_PER_DOT)], axis=0)
                y_quads.append(jnp.dot(jnp.concatenate(m_heads, axis=1), rhs, preferred_element_type=F32))
            y_g = jnp.concatenate(y_quads, axis=1) + y_off * ecum_w[:, gs]
            upd = jnp.dot(b_g.T.astype(BF16), xdd_b[:, gs], preferred_element_type=F32)
            s_new = s_g * cdec_w[:, gs] + upd
            s_scr[g] = s_new

            if sub == CHUNKS_PER_STEP - 1:
                @pl.when(c == n_steps - 1)
                def _(s_new=s_new, gs=gs):
                    st_ref[0, gs, :] = s_new.T

            zg = z_ref[rs, gs].astype(F32)
            y_scr[rs, gs] = (y_g + xs[:, gs] * dsk_ref[:, gs]) * (zg * jax.nn.sigmoid(zg))
    yn_ref[...] = _gated_norm(y_scr[...], g_ref[...]).astype(yn_ref.dtype)


def _ssd_prompt(z, xbc, dtp, conv_w, conv_b, layer, bias, alog, dsk_w, g_ssd, e2, nb, seq):
    rows_n = SSD_CHUNK * CHUNKS_PER_STEP
    ns = seq // rows_n
    body = functools.partial(_ssd_prompt_body, n_steps=ns)
    const = lambda b, c: (0, 0)
    rows = lambda b, c: (b * ns + c, 0)
    cb2 = conv_b.reshape(1, CONV_DIM)
    return pl.pallas_call(
        body,
        grid=(nb, ns),
        in_specs=[pl.BlockSpec((rows_n, SSD_INNER), rows),
                  pl.BlockSpec((rows_n, SSD_INNER), rows),
                  pl.BlockSpec((rows_n, BC_COLS), lambda b, c: (b * ns + c, SSD_INNER // BC_COLS)),
                  pl.BlockSpec((rows_n, LANES), rows),
                  pl.BlockSpec((None, CONV_W, SSD_INNER), lambda b, c: (layer, 0, 0)),
                  pl.BlockSpec((None, CONV_W, BC_COLS), lambda b, c: (layer, 0, SSD_INNER // BC_COLS)),
                  pl.BlockSpec((1, SSD_INNER), const),
                  pl.BlockSpec((1, BC_COLS), lambda b, c: (0, SSD_INNER // BC_COLS)),
                  pl.BlockSpec((1, LANES), const),
                  pl.BlockSpec((1, LANES), const),
                  pl.BlockSpec((1, SSD_INNER), const),
                  pl.BlockSpec((1, SSD_INNER), const),
                  pl.BlockSpec((2 * LANES, SSD_INNER), const)],
        out_specs=[pl.BlockSpec((rows_n, SSD_INNER), rows),
                   pl.BlockSpec((1, SSD_INNER, SSD_STATE), lambda b, c: (b, 0, 0))],
        out_shape=[jax.ShapeDtypeStruct((nb * seq, SSD_INNER), BF16),
                   jax.ShapeDtypeStruct((nb, SSD_INNER, SSD_STATE), F32)],
        scratch_shapes=[pltpu.VMEM((SSD_GROUPS, SSD_STATE, GROUP_COLS), F32),
                        pltpu.VMEM((rows_n, SSD_INNER), F32),
                        pltpu.VMEM((SUBLANES, SSD_INNER), F32),
                        pltpu.VMEM((SUBLANES, BC_COLS), F32)],
        compiler_params=_params(2, 56),
        name="ssd_prompt",
    )(z, xbc, xbc, dtp, conv_w, conv_w, cb2, cb2, bias, alog, dsk_w, g_ssd, e2)


SEQ_PER_STEP = 4


def _ssd_sample_body(xc_ref, z_ref, dtp_ref, h0_ref, bias_ref, alog_ref, dsk_ref, g_ref, e3_ref, *rest, steps):
    yn_ref, hn_ref = rest[-2:]
    rows_n = SEQ_PER_STEP * steps
    lane = lax.broadcasted_iota(jnp.int32, (rows_n, LANES), 1)
    row = lax.broadcasted_iota(jnp.int32, (rows_n, LANES), 0)
    tpos = row % steps
    head_ok = lane < SSD_HEADS

    def shift(v, j, tp):
        return jnp.where(tp >= j, pltpu.roll(v, j, axis=0), 0.0)

    dt = jnp.where(head_ok, _softplus(dtp_ref[...] + bias_ref[...]), 0.0)
    a = -jnp.exp(alog_ref[...])
    dta = dt * a
    cum = dta
    for j in range(1, steps):
        cum = cum + shift(dta, j, tpos)
    last = jnp.zeros_like(cum)
    for k in range(steps):
        back = steps - 1 - k
        src = cum if back == 0 else pltpu.roll(cum, rows_n - back, axis=0)
        last = jnp.where(tpos == k, src, last)
    ecum = jnp.exp(cum)
    dte = jnp.where(head_ok, jnp.exp(last - cum), 0.0)
    cdec = jnp.exp(last)

    b_all = xc_ref[:, SSD_INNER:SSD_INNER + SSD_GROUPS * SSD_STATE]
    c_all = xc_ref[:, SSD_INNER + SSD_GROUPS * SSD_STATE:]
    row_w = lax.broadcasted_iota(jnp.int32, b_all.shape, 0)
    tpos_w = row_w % steps
    coefs = []
    for j in range(steps):
        prod = c_all * (b_all if j == 0 else shift(b_all, j, tpos_w))
        cb = jnp.zeros((rows_n, LANES), F32)
        for g in range(SSD_GROUPS):
            cbg = jnp.sum(prod[:, g * SSD_STATE:(g + 1) * SSD_STATE], axis=1, keepdims=True)
            cb = jnp.where((lane // SSD_HPG) == g, cbg, cb)
        if j == 0:
            coefs.append(jnp.where(head_ok, cb, 0.0))
        else:
            seg = jnp.exp(cum - shift(cum, j, tpos))
            coefs.append(jnp.where(head_ok & (tpos >= j), cb * seg, 0.0))
    stack = jnp.concatenate([dt, dte, ecum, cdec] + coefs, axis=0)
    wide = jnp.dot(_split(stack, 3), e3_ref[...], preferred_element_type=F32)
    dt_w, dte_w = wide[:rows_n], wide[rows_n:2 * rows_n]
    ecum_w, cdec_w = wide[2 * rows_n:3 * rows_n], wide[3 * rows_n:4 * rows_n]

    xs = xc_ref[:, :SSD_INNER]
    xdt = xs * dt_w
    y = wide[4 * rows_n:5 * rows_n] * xdt
    for j in range(1, steps):
        y = y + wide[(4 + j) * rows_n:(5 + j) * rows_n] * pltpu.roll(xdt, j, axis=0)
    xdd = xdt * dte_w

    row_g = lax.broadcasted_iota(jnp.int32, (rows_n, GROUP_COLS), 0)
    ones_b = jnp.ones((rows_n, SSD_STATE), BF16)
    c_bf = c_all.astype(BF16)
    b_bf = b_all.astype(BF16)
    y_off_groups = []
    for g in range(SSD_GROUPS):
        gs = slice(g * GROUP_COLS, (g + 1) * GROUP_COLS)
        ns = slice(g * SSD_STATE, (g + 1) * SSD_STATE)
        y_off = jnp.zeros((rows_n, GROUP_COLS), F32)
        for i in range(SEQ_PER_STEP):
            mine = (row_g // steps) == i
            h0g = h0_ref[i, gs, :]
            yo = lax.dot_general(c_bf[:, ns], h0g.astype(BF16), NT_DIMS, preferred_element_type=F32)
            y_off = jnp.where(mine, yo, y_off)
            x_i = jnp.where(mine, xdd[:, gs], 0.0).astype(BF16)
            upd = lax.dot_general(x_i, b_bf[:, ns], (((0,), (0,)), ((), ())),
                                  preferred_element_type=F32)
            cd = cdec_w[i * steps:i * steps + 1, gs]
            cd_hi = cd.astype(BF16).astype(F32)
            cd_mid = (cd - cd_hi).astype(BF16).astype(F32)
            cd_lo = cd - cd_hi - cd_mid
            cd_rows = jnp.where(row_g == 0, cd_hi,
                                jnp.where(row_g == 1, cd_mid,
                                          jnp.where(row_g == 2, cd_lo, 0.0))).astype(BF16)
            cd_col = lax.dot_general(cd_rows, ones_b, (((0,), (0,)), ((), ())),
                                     preferred_element_type=F32)
            hn_ref[i, gs, :] = h0g * cd_col + upd
        y_off_groups.append(y_off)
    y = y + jnp.concatenate(y_off_groups, axis=1) * ecum_w
    zz = z_ref[...].astype(F32)
    y2 = (y + xs * dsk_ref[...]) * (zz * jax.nn.sigmoid(zz))
    yn_ref[...] = _gated_norm(y2, g_ref[...]).astype(yn_ref.dtype)


def _ssd_sample(xc, z, dtp, h_all, layer, new_states, bias, alog, dsk_w, g_ssd, e3, steps):
    nb = h_all.shape[1]
    rows_n = SEQ_PER_STEP * steps
    body = functools.partial(_ssd_sample_body, steps=steps)
    const = lambda i: (0, 0)
    state_spec = pl.BlockSpec((None, SEQ_PER_STEP, SSD_INNER, SSD_STATE), lambda i: (layer, i, 0, 0))
    in_specs = [pl.BlockSpec((rows_n, CONV_DIM), lambda i: (i, 0)),
                pl.BlockSpec((rows_n, SSD_INNER), lambda i: (i, 0)),
                pl.BlockSpec((rows_n, LANES), lambda i: (i, 0)),
                state_spec,
                pl.BlockSpec((1, LANES), const),
                pl.BlockSpec((1, LANES), const),
                pl.BlockSpec((1, SSD_INNER), const),
                pl.BlockSpec((1, SSD_INNER), const),
                pl.BlockSpec((3 * LANES, SSD_INNER), const)]
    args = [xc, z, dtp, h_all, bias, alog, dsk_w, g_ssd, e3]
    aliases = {}
    if new_states is not None:
        in_specs.append(pl.BlockSpec(memory_space=pl.ANY))
        args.append(new_states)
        aliases = {len(args) - 1: 1}
    return pl.pallas_call(
        body,
        grid=(nb // SEQ_PER_STEP,),
        in_specs=in_specs,
        out_specs=[pl.BlockSpec((rows_n, SSD_INNER), lambda i: (i, 0)), state_spec],
        out_shape=[jax.ShapeDtypeStruct((nb * steps, SSD_INNER), BF16),
                   jax.ShapeDtypeStruct(h_all.shape, F32)],
        input_output_aliases=aliases,
        compiler_params=_params(1, 56),
        name="ssd_sample",
    )(*args)


def _vnorm(v, g):
    v = jax.nn.gelu(v)
    r = lax.rsqrt(jnp.mean(v * v, axis=-1, keepdims=True) + RMS_EPS)
    return (v * r) * g


CMLP_CHUNKS_PER_STEP = 2


def _cmlp_prompt_body(u_ref, v_ref, ws_ref, bst_ref, gv_ref, cm_ref, vs_ref, *, n_steps):
    q = CMLP_CHUNK
    row = lax.broadcasted_iota(jnp.int32, (q, q), 0)
    col = lax.broadcasted_iota(jnp.int32, (q, q), 1)
    wms = [jnp.where(row >= col, ws_ref[g], 0.0).astype(BF16) for g in range(CMLP_GROUPS)]
    for sub in range(CMLP_CHUNKS_PER_STEP):
        rs = slice(sub * q, (sub + 1) * q)
        vn = _vnorm(v_ref[rs, :].astype(F32), gv_ref[...])
        if sub == CMLP_CHUNKS_PER_STEP - 1:
            @pl.when(pl.program_id(0) % n_steps == n_steps - 1)
            def _(vn=vn):
                vs_ref[0] = vn

        u = jax.nn.gelu(u_ref[rs, :].astype(F32))
        vb = vn.astype(BF16)
        for g in range(CMLP_GROUPS):
            gs = slice(g * CMLP_GROUP_DIM, (g + 1) * CMLP_GROUP_DIM)
            mixed = jnp.dot(wms[g], vb[:, gs], preferred_element_type=F32) + bst_ref[:, g:g + 1]
            cm_ref[rs, gs] = (u[:, gs] * mixed).astype(cm_ref.dtype)


def _cmlp_prompt(uvg, w_s, b_s, g_v, layer, nb, seq):
    q = CMLP_CHUNK
    rows_n = q * CMLP_CHUNKS_PER_STEP
    nc = seq // rows_n
    body = functools.partial(_cmlp_prompt_body, n_steps=nc)
    return pl.pallas_call(
        body,
        grid=(nb * nc,),
        in_specs=[pl.BlockSpec((rows_n, CMLP_WIDTH), lambda i: (i, 0)),
                  pl.BlockSpec((rows_n, CMLP_WIDTH), lambda i: (i, 1)),
                  pl.BlockSpec((None, CMLP_GROUPS, q, q), lambda i: (layer, 0, 0, 0)),
                  pl.BlockSpec((q, CMLP_GROUPS), lambda i: (0, 0)),
                  pl.BlockSpec((1, CMLP_WIDTH), lambda i: (0, 0))],
        out_specs=[pl.BlockSpec((rows_n, CMLP_WIDTH), lambda i: (i, 0)),
                   pl.BlockSpec((1, q, CMLP_WIDTH), lambda i: (i // nc, 0, 0))],
        out_shape=[jax.ShapeDtypeStruct((nb * seq, CMLP_WIDTH), BF16),
                   jax.ShapeDtypeStruct((nb, q, CMLP_WIDTH), F32)],
        compiler_params=_params(1, 32),
        name="cmlp_prompt",
    )(uvg, uvg, w_s, b_s.T, g_v.reshape(1, CMLP_WIDTH))


def _cmlp_sample_body(u_ref, v_ref, wrow_ref, brow_ref, gv_ref, cm_ref, vs_ref):
    steps = u_ref.shape[0]
    vn = [_vnorm(v_ref[t].astype(F32), gv_ref[...]) for t in range(steps)]
    for t in range(steps):
        vs_ref[t] = vn[t]
        mixed = brow_ref[t:t + 1, :] + wrow_ref[t, 0:1, :] * vn[0]
        for s in range(1, t + 1):
            mixed = mixed + wrow_ref[t, s:s + 1, :] * vn[s]
        cm_ref[t] = (jax.nn.gelu(u_ref[t].astype(F32)) * mixed).astype(cm_ref.dtype)


def _cmlp_sample(uvg_t, w_s, b_s, g_v, tb=32):
    steps, nb, _ = uvg_t.shape
    wrow = jnp.repeat(jnp.transpose(w_s[:, :steps, :steps], (1, 2, 0)), CMLP_GROUP_DIM, axis=-1)
    brow = jnp.repeat(b_s[:, :steps].T, CMLP_GROUP_DIM, axis=-1)
    return pl.pallas_call(
        _cmlp_sample_body,
        grid=(nb // tb,),
        in_specs=[pl.BlockSpec((steps, tb, CMLP_WIDTH), lambda i: (0, i, 0)),
                  pl.BlockSpec((steps, tb, CMLP_WIDTH), lambda i: (0, i, 1)),
                  pl.BlockSpec((steps, steps, CMLP_WIDTH), lambda i: (0, 0, 0)),
                  pl.BlockSpec((steps, CMLP_WIDTH), lambda i: (0, 0)),
                  pl.BlockSpec((1, CMLP_WIDTH), lambda i: (0, 0))],
        out_specs=[pl.BlockSpec((steps, tb, CMLP_WIDTH), lambda i: (0, i, 0)),
                   pl.BlockSpec((steps, tb, CMLP_WIDTH), lambda i: (0, i, 0))],
        out_shape=[jax.ShapeDtypeStruct((steps, nb, CMLP_WIDTH), BF16),
                   jax.ShapeDtypeStruct((steps, nb, CMLP_WIDTH), F32)],
        compiler_params=_params(1, 32),
        name="cmlp_sample",
    )(uvg_t, uvg_t, wrow, brow, g_v.reshape(1, CMLP_WIDTH))


def _layer(x, hg, ss, ple_all, w, gains, layer, mixers, big):
    tm = 1024 if big else 512
    mm = functools.partial(_fused_mm, layer=layer)
    w_in_t = w["w_in_t"]
    z = mm([hg], [Dot(0, w_in_t, 0, scaled=True, transposed=True)], [], _epi_first, ss=ss,
           n=OFF_Z, tm=tm, tn=1024, out_dtype=BF16, name="in_proj_z")
    xbc = mm([hg], [Dot(0, w_in_t, OFF_Z // 1024, scaled=True, transposed=True)], [], _epi_first, ss=ss,
             n=CONV_DIM, tm=tm, tn=1024, out_dtype=F32, name="in_proj_xbc")
    dtp = mm([hg], [Dot(0, w_in_t, OFF_XBC // LANES, scaled=True, transposed=True)], [], _epi_first, ss=ss,
             n=LANES, tm=tm, tn=LANES, out_dtype=F32, name="in_proj_dt")
    uvg = mm([hg], [Dot(0, w_in_t, OFF_XBC // 1024, scaled=True, transposed=True, shifted=True)], [],
             _epi_first, ss=ss, n=4 * D_MODEL, tm=tm, tn=1024, out_dtype=BF16, name="in_proj_uvg")
    yn, cm, states = mixers(z, xbc, dtp, uvg)
    tn = 512
    merged = mm([yn, cm], [Dot(0, w["w_br_a"]), Dot(1, w["w_br_b"])],
                [(uvg, 2 * D_MODEL // tn), (uvg, 3 * D_MODEL // tn)], _epi_gate_merge,
                n=D_MODEL, tm=512, tn=tn, out_dtype=BF16, name="branch_merge")
    x, hn, ss = mm([merged], [Dot(0, w["w_out"])], [(x, 0)], _epi_residual, next_gain=gains["g_ffn"],
                   n=D_MODEL, tm=tm, tn=1024, out_dtype=F32, name="out_proj")
    act = mm([hn], [Dot(0, w["w_gate_up"], 0, scaled=True), Dot(0, w["w_gate_up"], FFN_HIDDEN // tn, scaled=True)],
             [], _epi_swiglu, ss=ss, n=FFN_HIDDEN, tm=2 * tm, tn=tn, out_dtype=BF16, name="ffn_up")
    x, hp, ss = mm([act], [Dot(0, w["w_down"])], [(x, 0)], _epi_residual, next_gain=gains["g_ple"],
                   n=D_MODEL, tm=512, tn=tn, out_dtype=F32, name="ffn_down")
    ple_dots = [Dot(0, w["w_ple_gate"], 0, scaled=True), Dot(1, w["w_ple"])]
    if gains["g_next"] is None:
        x = mm([hp, ple_all], ple_dots, [(x, 0)], _epi_ple, ss=ss,
               n=D_MODEL, tm=512, tn=1024, out_dtype=F32, name="ple")
        return x, None, None, states
    x, hg, ss = mm([hp, ple_all], ple_dots, [(x, 0)], _epi_ple, ss=ss, next_gain=gains["g_next"],
                   n=D_MODEL, tm=512, tn=1024, out_dtype=F32, name="ple")
    return x, hg, ss, states


def kernel(x_prompt, x_sample, p_prompt, p_sample, state_ssd, state_conv, g_mix, w_in, conv_w, conv_b,
           dt_bias, a_log, d_skip, g_ssd, w_br_a, g_v, w_s, b_s, w_br_b, w_out, g_ffn, w_gate_up, w_down,
           g_ple, w_ple_gate, w_ple, g_final):
    nbp, seq, _ = x_prompt.shape
    nbs, steps, _ = x_sample.shape
    assert seq % (SSD_CHUNK * CHUNKS_PER_STEP) == 0 and seq % (CMLP_CHUNK * CMLP_CHUNKS_PER_STEP) == 0
    assert steps <= SSD_CHUNK

    head_rows = lax.broadcasted_iota(jnp.int32, (LANES, SSD_INNER), 0)
    head_cols = lax.broadcasted_iota(jnp.int32, (LANES, SSD_INNER), 1) // SSD_HEADDIM
    expand = (head_rows == head_cols).astype(BF16)
    e2, e3 = jnp.tile(expand, (2, 1)), jnp.tile(expand, (3, 1))
    pad_heads = lambda v: jnp.pad(v, (0, LANES - SSD_HEADS)).reshape(1, LANES)
    w_in_t = jnp.transpose(w_in, (0, 2, 1))
    w = dict(w_in_t=w_in_t, w_br_a=w_br_a, w_br_b=w_br_b, w_out=w_out,
             w_gate_up=w_gate_up, w_down=w_down, w_ple_gate=w_ple_gate, w_ple=w_ple)

    xp = x_prompt.reshape(nbp * seq, D_MODEL)
    xs = jnp.transpose(x_sample, (1, 0, 2)).reshape(steps * nbs, D_MODEL)
    ple_p = p_prompt.reshape(DEPTH, nbp * seq, PLE_DIM)
    ple_s = jnp.transpose(p_sample, (0, 2, 1, 3)).reshape(DEPTH, steps * nbs, PLE_DIM)
    conv_state_t = jnp.transpose(state_conv, (0, 2, 1, 3))
    h_all = state_ssd.reshape(DEPTH, nbs, SSD_INNER, SSD_STATE)
    new_states = None
    hgp, ssp = _prenorm(xp, g_mix[0])
    hgs, sss = _prenorm(xs, g_mix[0])
    outs = {k: [] for k in ("ssd_p", "conv_p", "v_p", "conv_s", "v_s")}
    for i in range(DEPTH):
        gains = dict(g_ffn=g_ffn[i], g_ple=g_ple[i], g_next=g_mix[i + 1] if i + 1 < DEPTH else None)
        bias, alog = pad_heads(dt_bias[i]), pad_heads(a_log[i])
        dsk_w = jnp.repeat(d_skip[i], SSD_HEADDIM).reshape(1, SSD_INNER)
        gs = g_ssd[i].reshape(1, SSD_INNER)

        def prompt_mixers(z, xbc, dtp, uvg, i=i, bias=bias, alog=alog, dsk_w=dsk_w, gs=gs):
            yn, st = _ssd_prompt(z, xbc, dtp, conv_w, conv_b[i], i, bias, alog, dsk_w, gs, e2, nbp, seq)
            cm, vst = _cmlp_prompt(uvg, w_s, b_s[i], g_v[i], i, nbp, seq)
            conv_new = xbc.reshape(nbp, seq, CONV_DIM)[:, seq - (CONV_W - 1):, :]
            return yn, cm, (st.reshape(nbp, SSD_HEADS, SSD_HEADDIM, SSD_STATE), conv_new, vst)

        def sample_mixers(z, xbc, dtp, uvg, i=i, bias=bias, alog=alog, dsk_w=dsk_w, gs=gs, new_states=new_states):
            xbc_t = xbc.reshape(steps, nbs, CONV_DIM)
            xc_t = _conv_sample(xbc_t, conv_state_t, conv_w, conv_b[i], i)
            seq_major = lambda t, wd: jnp.transpose(t.reshape(steps, nbs, wd), (1, 0, 2)).reshape(nbs * steps, wd)
            yn_b, st = _ssd_sample(seq_major(xc_t, CONV_DIM), seq_major(z, OFF_Z),
                                   seq_major(dtp, LANES), h_all, i, new_states,
                                   bias, alog, dsk_w, gs, e3, steps)
            yn = jnp.transpose(yn_b.reshape(nbs, steps, SSD_INNER), (1, 0, 2)).reshape(steps * nbs, SSD_INNER)
            cm_t, vn_t = _cmlp_sample(uvg.reshape(steps, nbs, 4 * D_MODEL), w_s[i], b_s[i], g_v[i])
            conv_new = jnp.transpose(xbc_t[steps - (CONV_W - 1):], (1, 0, 2))
            return (yn, cm_t.reshape(steps * nbs, CMLP_WIDTH),
                    (st, conv_new, jnp.transpose(vn_t, (1, 0, 2))))

        xp, hgp, ssp, (hp_, cp_, vp_) = _layer(xp, hgp, ssp, ple_p, w, gains, i, prompt_mixers, True)
        xs, hgs, sss, (new_states, cs_, vs_) = _layer(xs, hgs, sss, ple_s, w, gains, i, sample_mixers, False)
        outs["ssd_p"].append(hp_); outs["conv_p"].append(cp_); outs["v_p"].append(vp_)
        outs["conv_s"].append(cs_); outs["v_s"].append(vs_)

    y_prompt = _rownorm(xp, g_final, F32).reshape(nbp, seq, D_MODEL)
    y_sample = jnp.transpose(_rownorm(xs, g_final, F32).reshape(steps, nbs, D_MODEL), (1, 0, 2))
    ssd_s = new_states.reshape(DEPTH, nbs, SSD_HEADS, SSD_HEADDIM, SSD_STATE)
    return (y_prompt, y_sample, jnp.stack(outs["ssd_p"]), jnp.stack(outs["conv_p"]), jnp.stack(outs["v_p"]),
            ssd_s, jnp.stack(outs["conv_s"]), jnp.stack(outs["v_s"]))
```

```python
import functools
from typing import Any, NamedTuple

import jax
import jax.numpy as jnp
from jax import lax
from jax.experimental import pallas as pl
from jax.experimental.pallas import tpu as pltpu

F32 = jnp.float32
BF16 = jnp.bfloat16

D_MODEL = 2048
DEPTH = 4
PLE_DIM = 256
RMS_EPS = 1e-6
SSD_INNER = 4096
SSD_HEADDIM = 64
SSD_HEADS = 64
SSD_GROUPS = 8
SSD_HPG = 8
SSD_STATE = 128
SSD_CHUNK = 128
CONV_W = 4
CONV_DIM = 6144
BC_COLS = 2 * SSD_GROUPS * SSD_STATE
CMLP_WIDTH = 2048
CMLP_GROUPS = 16
CMLP_GROUP_DIM = 128
CMLP_CHUNK = 128
FFN_HIDDEN = 5632
OFF_Z = SSD_INNER
OFF_XBC = OFF_Z + CONV_DIM
OFF_DT = OFF_XBC + SSD_HEADS
IN_WIDTH = OFF_DT + 4 * D_MODEL
GROUP_COLS = SSD_HPG * SSD_HEADDIM
LANES = 128
SUBLANES = 8
HALF_LANES = LANES // 2
VMEM_LIMIT_MB = 58
WIDE_TILE_VMEM_MB = 62
NT_DIMS = (((1,), (1,)), ((), ()))


def _params(n_axes, vmem_mb=VMEM_LIMIT_MB):
    return pltpu.CompilerParams(dimension_semantics=("arbitrary",) * n_axes,
                                vmem_limit_bytes=vmem_mb * 2 ** 20)


def _softplus(x):
    return jnp.maximum(x, 0.0) + jnp.log1p(jnp.exp(-jnp.abs(x)))


def _split(q, parts):
    out, rem = [], q
    for _ in range(parts):
        piece = rem.astype(BF16)
        out.append(piece)
        rem = rem - piece.astype(F32)
    return jnp.concatenate(out, axis=1)


def _inv_rms(ss_parts, width):
    return lax.rsqrt(jnp.sum(ss_parts, axis=0) / width + RMS_EPS)


def _rownorm_body(x_ref, g_ref, o_ref):
    xf = x_ref[...]
    r = lax.rsqrt(jnp.mean(xf * xf, axis=-1, keepdims=True) + RMS_EPS)
    o_ref[...] = ((xf * r) * g_ref[...]).astype(o_ref.dtype)


def _rownorm(x, g, out_dtype, tr=512):
    m, d = x.shape
    return pl.pallas_call(
        _rownorm_body,
        grid=(m // tr,),
        in_specs=[pl.BlockSpec((tr, d), lambda i: (i, 0)),
                  pl.BlockSpec((1, d), lambda i: (0, 0))],
        out_specs=pl.BlockSpec((tr, d), lambda i: (i, 0)),
        out_shape=jax.ShapeDtypeStruct((m, d), out_dtype),
        compiler_params=_params(1, 32),
        name="rownorm",
    )(x, g.reshape(1, d))


def _prenorm_body(x_ref, g_ref, xg_ref, ss_ref):
    xf = x_ref[...]
    xg_ref[...] = (xf * g_ref[...]).astype(xg_ref.dtype)
    ss_ref[...] = jnp.sum(xf * xf, axis=-1, keepdims=True)


def _prenorm(x, g, tr=512):
    m, d = x.shape
    return pl.pallas_call(
        _prenorm_body,
        grid=(m // tr,),
        in_specs=[pl.BlockSpec((tr, d), lambda i: (i, 0)),
                  pl.BlockSpec((1, d), lambda i: (0, 0))],
        out_specs=[pl.BlockSpec((tr, d), lambda i: (i, 0)),
                   pl.BlockSpec((None, tr, 1), lambda i: (0, i, 0))],
        out_shape=[jax.ShapeDtypeStruct((m, d), BF16), jax.ShapeDtypeStruct((1, m, 1), F32)],
        compiler_params=_params(1, 32),
        name="prenorm",
    )(x, g.reshape(1, d))


class Dot(NamedTuple):
    act: int
    w: Any
    off: int = 0
    scaled: bool = False
    transposed: bool = False
    shifted: bool = False


def _fused_mm_body(*refs, n_acts, dots, n_extras, epi, has_ss, has_gain):
    a_refs = refs[:n_acts]
    pos = n_acts
    w_refs = []
    for d in dots:
        w_refs.append(refs[pos:pos + (2 if d.shifted else 1)])
        pos += 2 if d.shifted else 1
    ss_ref = refs[pos] if has_ss else None
    pos += has_ss
    e_refs = refs[pos:pos + n_extras]
    pos += n_extras
    gain_ref = refs[pos] if has_gain else None
    pos += has_gain
    o_ref = refs[pos]
    pos += 1
    if has_gain:
        xg_ref, ssq_ref = refs[pos:pos + 2]
        pos += 2
    scr = refs[pos:]

    @pl.when(pl.program_id(1) == 0)
    def _():
        for d, wr, s in zip(dots, w_refs, scr):
            if d.shifted:
                keep = s.shape[0] - HALF_LANES
                s[:keep, :] = wr[0][HALF_LANES:, :].astype(BF16)
                s[keep:, :] = wr[1][...].astype(BF16)
            else:
                s[...] = wr[0][...].astype(BF16)

    acts = [a[...].astype(BF16) for a in a_refs]
    r = _inv_rms(ss_ref[...], D_MODEL) if has_ss else None
    accs = []
    for d, s in zip(dots, scr):
        if d.transposed:
            acc = lax.dot_general(acts[d.act], s[...], NT_DIMS, preferred_element_type=F32)
        else:
            acc = jnp.dot(acts[d.act], s[...], preferred_element_type=F32)
        accs.append(acc * r if d.scaled else acc)
    out = epi(accs, [e[...] for e in e_refs])
    o_ref[...] = out.astype(o_ref.dtype)
    if has_gain:
        xg_ref[...] = (out * gain_ref[...]).astype(xg_ref.dtype)
        ssq_ref[...] = jnp.sum(out * out, axis=-1, keepdims=True)


def _fused_mm(acts, dots, extras, epi, *, layer, n, tm, tn, out_dtype, name, ss=None, next_gain=None,
              vmem_mb=VMEM_LIMIT_MB):
    m = acts[0].shape[-2]
    tm = min(tm, m)
    in_specs, args, scratch = [], [], []
    for a in acts:
        if a.ndim == 3:
            in_specs.append(pl.BlockSpec((None, tm, a.shape[2]), lambda j, i: (layer, i, 0)))
        else:
            in_specs.append(pl.BlockSpec((tm, a.shape[1]), lambda j, i: (i, 0)))
        args.append(a)
    for d in dots:
        k = acts[d.act].shape[-1]
        if d.transposed:
            in_specs.append(pl.BlockSpec((None, tn, k), lambda j, i, off=d.off: (layer, j + off, 0)))
            args.append(d.w)
            if d.shifted:
                per = tn // HALF_LANES
                in_specs.append(pl.BlockSpec((None, HALF_LANES, k),
                                             lambda j, i, off=d.off, per=per: (layer, (j + off + 1) * per, 0)))
                args.append(d.w)
            scratch.append(pltpu.VMEM((tn, k), BF16))
        else:
            in_specs.append(pl.BlockSpec((None, k, tn), lambda j, i, off=d.off: (layer, 0, j + off)))
            args.append(d.w)
            scratch.append(pltpu.VMEM((k, tn), BF16))
    if ss is not None:
        in_specs.append(pl.BlockSpec((ss.shape[0], tm, 1), lambda j, i: (0, i, 0)))
        args.append(ss)
    for e, off in extras:
        in_specs.append(pl.BlockSpec((tm, tn), lambda j, i, off=off: (i, j + off)))
        args.append(e)
    out_specs = [pl.BlockSpec((tm, tn), lambda j, i: (i, j))]
    out_shape = [jax.ShapeDtypeStruct((m, n), out_dtype)]
    if next_gain is not None:
        in_specs.append(pl.BlockSpec((1, tn), lambda j, i: (0, j)))
        args.append(next_gain.reshape(1, n))
        out_specs += [pl.BlockSpec((tm, tn), lambda j, i: (i, j)),
                      pl.BlockSpec((None, tm, 1), lambda j, i: (j, i, 0))]
        out_shape += [jax.ShapeDtypeStruct((m, n), BF16), jax.ShapeDtypeStruct((n // tn, m, 1), F32)]
    body = functools.partial(_fused_mm_body, n_acts=len(acts), dots=tuple(d._replace(w=None) for d in dots),
                             n_extras=len(extras), epi=epi, has_ss=ss is not None,
                             has_gain=next_gain is not None)
    res = pl.pallas_call(
        body,
        grid=(n // tn, m // tm),
        in_specs=in_specs,
        out_specs=out_specs,
        out_shape=out_shape,
        scratch_shapes=scratch,
        compiler_params=_params(2, vmem_mb),
        name=name,
    )(*args)
    return res if next_gain is not None else res[0]


def _epi_first(accs, extras):
    return accs[0]


def _epi_residual(accs, extras):
    return extras[0] + accs[0]


def _epi_gate_merge(accs, extras):
    return (jax.nn.sigmoid(extras[0].astype(F32)) * accs[0]
            + jax.nn.sigmoid(extras[1].astype(F32)) * accs[1])


def _epi_swiglu(accs, extras):
    return jax.nn.silu(accs[0]) * accs[1]


def _epi_ple(accs, extras):
    return extras[0] + jax.nn.sigmoid(accs[0]) * accs[1]


def _conv_silu_rows(raw_ref, prev_rows, w_ref, b_ref):
    n_rows = raw_ref.shape[0]
    head = raw_ref[:SUBLANES, :]
    ext = jnp.concatenate([prev_rows, head], axis=0)
    acc_head = b_ref[...] + w_ref[CONV_W - 1:CONV_W, :] * head
    acc_rest = b_ref[...] + w_ref[CONV_W - 1:CONV_W, :] * raw_ref[SUBLANES:, :]
    for j in range(1, CONV_W):
        wj = w_ref[CONV_W - 1 - j:CONV_W - j, :]
        acc_head = acc_head + wj * pltpu.roll(ext, j, axis=0)[SUBLANES:, :]
        acc_rest = acc_rest + wj * raw_ref[pl.ds(SUBLANES - j, n_rows - SUBLANES), :]
    acc = jnp.concatenate([acc_head, acc_rest], axis=0)
    return acc * jax.nn.sigmoid(acc)


def _conv_sample_body(x_ref, s_ref, w_ref, b_ref, o_ref):
    steps = x_ref.shape[0]
    xpad = [s_ref[k] for k in range(CONV_W - 1)] + [x_ref[t] for t in range(steps)]
    for t in range(steps):
        acc = b_ref[...] + w_ref[0:1, :] * xpad[t]
        for k in range(1, CONV_W):
            acc = acc + w_ref[k:k + 1, :] * xpad[t + k]
        o_ref[t] = acc * jax.nn.sigmoid(acc)


def _conv_sample(xbc_t, state_t, conv_w, conv_b, layer, tc=1024):
    steps, nb, _ = xbc_t.shape
    return pl.pallas_call(
        _conv_sample_body,
        grid=(CONV_DIM // tc,),
        in_specs=[pl.BlockSpec((steps, nb, tc), lambda c: (0, 0, c)),
                  pl.BlockSpec((None, CONV_W - 1, nb, tc), lambda c: (layer, 0, 0, c)),
                  pl.BlockSpec((None, CONV_W, tc), lambda c: (layer, 0, c)),
                  pl.BlockSpec((1, tc), lambda c: (0, c))],
        out_specs=pl.BlockSpec((steps, nb, tc), lambda c: (0, 0, c)),
        out_shape=jax.ShapeDtypeStruct((steps, nb, CONV_DIM), F32),
        compiler_params=_params(1, 32),
        name="conv_sample",
    )(xbc_t, state_t, conv_w, conv_b.reshape(1, CONV_DIM))


HEADS_PER_DOT = 4
QUAD_COLS = HEADS_PER_DOT * SSD_HEADDIM
CHUNKS_PER_STEP = 2


def _gated_norm(y2, g):
    r = lax.rsqrt(jnp.mean(y2 * y2, axis=-1, keepdims=True) + RMS_EPS)
    return (y2 * r) * g


def _ssd_prompt_body(z_ref, xr_ref, bcr_ref, dtp_ref, cwx_ref, cwb_ref, cbx_ref, cbb_ref,
                     bias_ref, alog_ref, dsk_ref, g_ref, e2_ref,
                     yn_ref, st_ref, s_scr, y_scr, prev_x, prev_bc, *, n_steps):
    c = pl.program_id(1)
    q = SSD_CHUNK
    rows_n = q * CHUNKS_PER_STEP

    @pl.when(c == 0)
    def _():
        s_scr[...] = jnp.zeros_like(s_scr)
        prev_x[...] = jnp.zeros_like(prev_x)
        prev_bc[...] = jnp.zeros_like(prev_bc)

    xs_all = _conv_silu_rows(xr_ref, prev_x[...], cwx_ref, cbx_ref)
    bc_all = _conv_silu_rows(bcr_ref, prev_bc[...], cwb_ref, cbb_ref)
    prev_x[...] = xr_ref[rows_n - SUBLANES:, :]
    prev_bc[...] = bcr_ref[rows_n - SUBLANES:, :]

    lane = lax.broadcasted_iota(jnp.int32, (q, LANES), 1)
    row = lax.broadcasted_iota(jnp.int32, (q, LANES), 0)
    head_ok = lane < SSD_HEADS
    causal = row >= lane
    tri = causal.astype(BF16)
    lane_q = lax.broadcasted_iota(jnp.int32, (q, QUAD_COLS), 1) // SSD_HEADDIM
    head_masks = [(lane_q == k).astype(BF16) for k in range(HEADS_PER_DOT)]
    a = -jnp.exp(alog_ref[...])

    for sub in range(CHUNKS_PER_STEP):
        rs = slice(sub * q, (sub + 1) * q)
        dt = jnp.where(head_ok, _softplus(dtp_ref[rs, :] + bias_ref[...]), 0.0)
        dta = dt * a
        cum3 = jnp.dot(tri, _split(dta, 3), preferred_element_type=F32)
        cum = cum3[:, :LANES] + cum3[:, LANES:2 * LANES] + cum3[:, 2 * LANES:]
        cum_t = (cum - jnp.where(head_ok, jnp.log(dt), 0.0)).T
        last = cum[q - 1:q, :]
        ecum = jnp.exp(cum)
        dtw = jnp.where(head_ok, dt * jnp.exp(last - cum), 0.0)
        stack = jnp.concatenate([dtw, ecum], axis=0)
        wide = jnp.dot(_split(stack, 2), e2_ref[...], preferred_element_type=F32)
        dtw_w, ecum_w = wide[:q], wide[q:]
        cdec_w = ecum_w[q - 1:q, :]

        xs = xs_all[rs]
        xs_b = xs.astype(BF16)
        xdd_b = (xs * dtw_w).astype(BF16)
        for g in range(SSD_GROUPS):
            gs = slice(g * GROUP_COLS, (g + 1) * GROUP_COLS)
            b_g = bc_all[rs, g * SSD_STATE:(g + 1) * SSD_STATE]
            c_g = bc_all[rs, (SSD_GROUPS + g) * SSD_STATE:(SSD_GROUPS + g + 1) * SSD_STATE]
            c_b = c_g.astype(BF16)
            cb = lax.dot_general(c_b, b_g.astype(BF16), NT_DIMS, preferred_element_type=F32)
            s_g = s_scr[g]
            y_off = jnp.dot(c_b, s_g.astype(BF16), preferred_element_type=F32)
            y_quads = []
            for quad in range(SSD_HPG // HEADS_PER_DOT):
                h0 = g * SSD_HPG + quad * HEADS_PER_DOT
                m_heads = []
                for k in range(HEADS_PER_DOT):
                    h = h0 + k
                    diff = cum[:, h:h + 1] - cum_t[h:h + 1, :]
                    seg = jnp.exp(jnp.where(causal, diff, -jnp.inf))
                    m_heads.append((cb * seg).astype(BF16))
                xq = xs_b[:, h0 * SSD_HEADDIM:h0 * SSD_HEADDIM + QUAD_COLS]
                rhs = jnp.concatenate([xq * head_masks[k] for k in range(HEADS_PER_DOT)], axis=0)
                y_quads.append(jnp.dot(jnp.concatenate(m_heads, axis=1), rhs, preferred_element_type=F32))
            y_g = jnp.concatenate(y_quads, axis=1) + y_off * ecum_w[:, gs]
            upd = jnp.dot(b_g.T.astype(BF16), xdd_b[:, gs], preferred_element_type=F32)
            s_new = s_g * cdec_w[:, gs] + upd
            s_scr[g] = s_new

            if sub == CHUNKS_PER_STEP - 1:
                @pl.when(c == n_steps - 1)
                def _(s_new=s_new, gs=gs):
                    st_ref[0, gs, :] = s_new.T

            zg = z_ref[rs, gs].astype(F32)
            y_scr[rs, gs] = (y_g + xs[:, gs] * dsk_ref[:, gs]) * (zg * jax.nn.sigmoid(zg))
    yn_ref[...] = _gated_norm(y_scr[...], g_ref[...]).astype(yn_ref.dtype)


def _ssd_prompt(z, xbc, dtp, conv_w, conv_b, layer, bias, alog, dsk_w, g_ssd, e2, nb, seq):
    rows_n = SSD_CHUNK * CHUNKS_PER_STEP
    ns = seq // rows_n
    body = functools.partial(_ssd_prompt_body, n_steps=ns)
    const = lambda b, c: (0, 0)
    rows = lambda b, c: (b * ns + c, 0)
    cb2 = conv_b.reshape(1, CONV_DIM)
    return pl.pallas_call(
        body,
        grid=(nb, ns),
        in_specs=[pl.BlockSpec((rows_n, SSD_INNER), rows),
                  pl.BlockSpec((rows_n, SSD_INNER), rows),
                  pl.BlockSpec((rows_n, BC_COLS), lambda b, c: (b * ns + c, SSD_INNER // BC_COLS)),
                  pl.BlockSpec((rows_n, LANES), rows),
                  pl.BlockSpec((None, CONV_W, SSD_INNER), lambda b, c: (layer, 0, 0)),
                  pl.BlockSpec((None, CONV_W, BC_COLS), lambda b, c: (layer, 0, SSD_INNER // BC_COLS)),
                  pl.BlockSpec((1, SSD_INNER), const),
                  pl.BlockSpec((1, BC_COLS), lambda b, c: (0, SSD_INNER // BC_COLS)),
                  pl.BlockSpec((1, LANES), const),
                  pl.BlockSpec((1, LANES), const),
                  pl.BlockSpec((1, SSD_INNER), const),
                  pl.BlockSpec((1, SSD_INNER), const),
                  pl.BlockSpec((2 * LANES, SSD_INNER), const)],
        out_specs=[pl.BlockSpec((rows_n, SSD_INNER), rows),
                   pl.BlockSpec((1, SSD_INNER, SSD_STATE), lambda b, c: (b, 0, 0))],
        out_shape=[jax.ShapeDtypeStruct((nb * seq, SSD_INNER), BF16),
                   jax.ShapeDtypeStruct((nb, SSD_INNER, SSD_STATE), F32)],
        scratch_shapes=[pltpu.VMEM((SSD_GROUPS, SSD_STATE, GROUP_COLS), F32),
                        pltpu.VMEM((rows_n, SSD_INNER), F32),
                        pltpu.VMEM((SUBLANES, SSD_INNER), F32),
                        pltpu.VMEM((SUBLANES, BC_COLS), F32)],
        compiler_params=_params(2, 56),
        name="ssd_prompt",
    )(z, xbc, xbc, dtp, conv_w, conv_w, cb2, cb2, bias, alog, dsk_w, g_ssd, e2)


SEQ_PER_STEP = 4


def _ssd_sample_body(xc_ref, z_ref, dtp_ref, h0_ref, bias_ref, alog_ref, dsk_ref, g_ref, e3_ref, *rest, steps):
    yn_ref, hn_ref = rest[-2:]
    rows_n = SEQ_PER_STEP * steps
    lane = lax.broadcasted_iota(jnp.int32, (rows_n, LANES), 1)
    row = lax.broadcasted_iota(jnp.int32, (rows_n, LANES), 0)
    tpos = row % steps
    head_ok = lane < SSD_HEADS

    def shift(v, j, tp):
        return jnp.where(tp >= j, pltpu.roll(v, j, axis=0), 0.0)

    dt = jnp.where(head_ok, _softplus(dtp_ref[...] + bias_ref[...]), 0.0)
    a = -jnp.exp(alog_ref[...])
    dta = dt * a
    cum = dta
    for j in range(1, steps):
        cum = cum + shift(dta, j, tpos)
    last = jnp.zeros_like(cum)
    for k in range(steps):
        back = steps - 1 - k
        src = cum if back == 0 else pltpu.roll(cum, rows_n - back, axis=0)
        last = jnp.where(tpos == k, src, last)
    ecum = jnp.exp(cum)
    dte = jnp.where(head_ok, jnp.exp(last - cum), 0.0)
    cdec = jnp.exp(last)

    b_all = xc_ref[:, SSD_INNER:SSD_INNER + SSD_GROUPS * SSD_STATE]
    c_all = xc_ref[:, SSD_INNER + SSD_GROUPS * SSD_STATE:]
    row_w = lax.broadcasted_iota(jnp.int32, b_all.shape, 0)
    tpos_w = row_w % steps
    coefs = []
    for j in range(steps):
        prod = c_all * (b_all if j == 0 else shift(b_all, j, tpos_w))
        cb = jnp.zeros((rows_n, LANES), F32)
        for g in range(SSD_GROUPS):
            cbg = jnp.sum(prod[:, g * SSD_STATE:(g + 1) * SSD_STATE], axis=1, keepdims=True)
            cb = jnp.where((lane // SSD_HPG) == g, cbg, cb)
        if j == 0:
            coefs.append(jnp.where(head_ok, cb, 0.0))
        else:
            seg = jnp.exp(cum - shift(cum, j, tpos))
            coefs.append(jnp.where(head_ok & (tpos >= j), cb * seg, 0.0))
    stack = jnp.concatenate([dt, dte, ecum, cdec] + coefs, axis=0)
    wide = jnp.dot(_split(stack, 3), e3_ref[...], preferred_element_type=F32)
    dt_w, dte_w = wide[:rows_n], wide[rows_n:2 * rows_n]
    ecum_w, cdec_w = wide[2 * rows_n:3 * rows_n], wide[3 * rows_n:4 * rows_n]

    xs = xc_ref[:, :SSD_INNER]
    xdt = xs * dt_w
    y = wide[4 * rows_n:5 * rows_n] * xdt
    for j in range(1, steps):
        y = y + wide[(4 + j) * rows_n:(5 + j) * rows_n] * pltpu.roll(xdt, j, axis=0)
    xdd = xdt * dte_w

    row_g = lax.broadcasted_iota(jnp.int32, (rows_n, GROUP_COLS), 0)
    ones_b = jnp.ones((rows_n, SSD_STATE), BF16)
    c_bf = c_all.astype(BF16)
    b_bf = b_all.astype(BF16)
    y_off_groups = []
    for g in range(SSD_GROUPS):
        gs = slice(g * GROUP_COLS, (g + 1) * GROUP_COLS)
        ns = slice(g * SSD_STATE, (g + 1) * SSD_STATE)
        y_off = jnp.zeros((rows_n, GROUP_COLS), F32)
        for i in range(SEQ_PER_STEP):
            mine = (row_g // steps) == i
            h0g = h0_ref[i, gs, :]
            yo = lax.dot_general(c_bf[:, ns], h0g.astype(BF16), NT_DIMS, preferred_element_type=F32)
            y_off = jnp.where(mine, yo, y_off)
            x_i = jnp.where(mine, xdd[:, gs], 0.0).astype(BF16)
            upd = lax.dot_general(x_i, b_bf[:, ns], (((0,), (0,)), ((), ())),
                                  preferred_element_type=F32)
            cd = cdec_w[i * steps:i * steps + 1, gs]
            cd_hi = cd.astype(BF16).astype(F32)
            cd_mid = (cd - cd_hi).astype(BF16).astype(F32)
            cd_lo = cd - cd_hi - cd_mid
            cd_rows = jnp.where(row_g == 0, cd_hi,
                                jnp.where(row_g == 1, cd_mid,
                                          jnp.where(row_g == 2, cd_lo, 0.0))).astype(BF16)
            cd_col = lax.dot_general(cd_rows, ones_b, (((0,), (0,)), ((), ())),
                                     preferred_element_type=F32)
            hn_ref[i, gs, :] = h0g * cd_col + upd
        y_off_groups.append(y_off)
    y = y + jnp.concatenate(y_off_groups, axis=1) * ecum_w
    zz = z_ref[...].astype(F32)
    y2 = (y + xs * dsk_ref[...]) * (zz * jax.nn.sigmoid(zz))
    yn_ref[...] = _gated_norm(y2, g_ref[...]).astype(yn_ref.dtype)


def _ssd_sample(xc, z, dtp, h_all, layer, new_states, bias, alog, dsk_w, g_ssd, e3, steps):
    nb = h_all.shape[1]
    rows_n = SEQ_PER_STEP * steps
    body = functools.partial(_ssd_sample_body, steps=steps)
    const = lambda i: (0, 0)
    state_spec = pl.BlockSpec((None, SEQ_PER_STEP, SSD_INNER, SSD_STATE), lambda i: (layer, i, 0, 0))
    in_specs = [pl.BlockSpec((rows_n, CONV_DIM), lambda i: (i, 0)),
                pl.BlockSpec((rows_n, SSD_INNER), lambda i: (i, 0)),
                pl.BlockSpec((rows_n, LANES), lambda i: (i, 0)),
                state_spec,
                pl.BlockSpec((1, LANES), const),
                pl.BlockSpec((1, LANES), const),
                pl.BlockSpec((1, SSD_INNER), const),
                pl.BlockSpec((1, SSD_INNER), const),
                pl.BlockSpec((3 * LANES, SSD_INNER), const)]
    args = [xc, z, dtp, h_all, bias, alog, dsk_w, g_ssd, e3]
    aliases = {}
    if new_states is not None:
        in_specs.append(pl.BlockSpec(memory_space=pl.ANY))
        args.append(new_states)
        aliases = {len(args) - 1: 1}
    return pl.pallas_call(
        body,
        grid=(nb // SEQ_PER_STEP,),
        in_specs=in_specs,
        out_specs=[pl.BlockSpec((rows_n, SSD_INNER), lambda i: (i, 0)), state_spec],
        out_shape=[jax.ShapeDtypeStruct((nb * steps, SSD_INNER), BF16),
                   jax.ShapeDtypeStruct(h_all.shape, F32)],
        input_output_aliases=aliases,
        compiler_params=_params(1, 56),
        name="ssd_sample",
    )(*args)


def _vnorm(v, g):
    v = jax.nn.gelu(v)
    r = lax.rsqrt(jnp.mean(v * v, axis=-1, keepdims=True) + RMS_EPS)
    return (v * r) * g


CMLP_CHUNKS_PER_STEP = 4


def _cmlp_prompt_body(u_ref, v_ref, ws_ref, bst_ref, gv_ref, cm_ref, vs_ref, *, n_steps):
    q = CMLP_CHUNK
    row = lax.broadcasted_iota(jnp.int32, (q, q), 0)
    col = lax.broadcasted_iota(jnp.int32, (q, q), 1)
    wms = [jnp.where(row >= col, ws_ref[g], 0.0).astype(BF16) for g in range(CMLP_GROUPS)]
    for sub in range(CMLP_CHUNKS_PER_STEP):
        rs = slice(sub * q, (sub + 1) * q)
        vn = _vnorm(v_ref[rs, :].astype(F32), gv_ref[...])
        if sub == CMLP_CHUNKS_PER_STEP - 1:
            @pl.when(pl.program_id(0) % n_steps == n_steps - 1)
            def _(vn=vn):
                vs_ref[0] = vn

        u = jax.nn.gelu(u_ref[rs, :].astype(F32))
        vb = vn.astype(BF16)
        for g in range(CMLP_GROUPS):
            gs = slice(g * CMLP_GROUP_DIM, (g + 1) * CMLP_GROUP_DIM)
            mixed = jnp.dot(wms[g], vb[:, gs], preferred_element_type=F32) + bst_ref[:, g:g + 1]
            cm_ref[rs, gs] = (u[:, gs] * mixed).astype(cm_ref.dtype)


def _cmlp_prompt(uvg, w_s, b_s, g_v, layer, nb, seq):
    q = CMLP_CHUNK
    rows_n = q * CMLP_CHUNKS_PER_STEP
    nc = seq // rows_n
    body = functools.partial(_cmlp_prompt_body, n_steps=nc)
    return pl.pallas_call(
        body,
        grid=(nb * nc,),
        in_specs=[pl.BlockSpec((rows_n, CMLP_WIDTH), lambda i: (i, 0)),
                  pl.BlockSpec((rows_n, CMLP_WIDTH), lambda i: (i, 1)),
                  pl.BlockSpec((None, CMLP_GROUPS, q, q), lambda i: (layer, 0, 0, 0)),
                  pl.BlockSpec((q, CMLP_GROUPS), lambda i: (0, 0)),
                  pl.BlockSpec((1, CMLP_WIDTH), lambda i: (0, 0))],
        out_specs=[pl.BlockSpec((rows_n, CMLP_WIDTH), lambda i: (i, 0)),
                   pl.BlockSpec((1, q, CMLP_WIDTH), lambda i: (i // nc, 0, 0))],
        out_shape=[jax.ShapeDtypeStruct((nb * seq, CMLP_WIDTH), BF16),
                   jax.ShapeDtypeStruct((nb, q, CMLP_WIDTH), F32)],
        compiler_params=_params(1, 32),
        name="cmlp_prompt",
    )(uvg, uvg, w_s, b_s.T, g_v.reshape(1, CMLP_WIDTH))


def _cmlp_sample_body(u_ref, v_ref, wrow_ref, brow_ref, gv_ref, cm_ref, vs_ref):
    steps = u_ref.shape[0]
    vn = [_vnorm(v_ref[t].astype(F32), gv_ref[...]) for t in range(steps)]
    for t in range(steps):
        vs_ref[t] = vn[t]
        mixed = brow_ref[t:t + 1, :] + wrow_ref[t, 0:1, :] * vn[0]
        for s in range(1, t + 1):
            mixed = mixed + wrow_ref[t, s:s + 1, :] * vn[s]
        cm_ref[t] = (jax.nn.gelu(u_ref[t].astype(F32)) * mixed).astype(cm_ref.dtype)


def _cmlp_sample(uvg_t, w_s, b_s, g_v, tb=32):
    steps, nb, _ = uvg_t.shape
    wrow = jnp.repeat(jnp.transpose(w_s[:, :steps, :steps], (1, 2, 0)), CMLP_GROUP_DIM, axis=-1)
    brow = jnp.repeat(b_s[:, :steps].T, CMLP_GROUP_DIM, axis=-1)
    return pl.pallas_call(
        _cmlp_sample_body,
        grid=(nb // tb,),
        in_specs=[pl.BlockSpec((steps, tb, CMLP_WIDTH), lambda i: (0, i, 0)),
                  pl.BlockSpec((steps, tb, CMLP_WIDTH), lambda i: (0, i, 1)),
                  pl.BlockSpec((steps, steps, CMLP_WIDTH), lambda i: (0, 0, 0)),
                  pl.BlockSpec((steps, CMLP_WIDTH), lambda i: (0, 0)),
                  pl.BlockSpec((1, CMLP_WIDTH), lambda i: (0, 0))],
        out_specs=[pl.BlockSpec((steps, tb, CMLP_WIDTH), lambda i: (0, i, 0)),
                   pl.BlockSpec((steps, tb, CMLP_WIDTH), lambda i: (0, i, 0))],
        out_shape=[jax.ShapeDtypeStruct((steps, nb, CMLP_WIDTH), BF16),
                   jax.ShapeDtypeStruct((steps, nb, CMLP_WIDTH), F32)],
        compiler_params=_params(1, 32),
        name="cmlp_sample",
    )(uvg_t, uvg_t, wrow, brow, g_v.reshape(1, CMLP_WIDTH))


def _layer(x, hg, ss, ple_all, w, gains, layer, mixers, big):
    tm = 1024 if big else 512
    mm = functools.partial(_fused_mm, layer=layer)
    w_in_t = w["w_in_t"]
    z = mm([hg], [Dot(0, w_in_t, 0, scaled=True, transposed=True)], [], _epi_first, ss=ss,
           n=OFF_Z, tm=2 * tm, tn=1024, out_dtype=BF16, vmem_mb=WIDE_TILE_VMEM_MB, name="in_proj_z")
    xbc = mm([hg], [Dot(0, w_in_t, OFF_Z // 1024, scaled=True, transposed=True)], [], _epi_first, ss=ss,
             n=CONV_DIM, tm=tm, tn=1024, out_dtype=F32, name="in_proj_xbc")
    dtp = mm([hg], [Dot(0, w_in_t, OFF_XBC // LANES, scaled=True, transposed=True)], [], _epi_first, ss=ss,
             n=LANES, tm=tm, tn=LANES, out_dtype=F32, name="in_proj_dt")
    uvg = mm([hg], [Dot(0, w_in_t, OFF_XBC // 1024, scaled=True, transposed=True, shifted=True)], [],
             _epi_first, ss=ss, n=4 * D_MODEL, tm=2 * tm, tn=1024, out_dtype=BF16, vmem_mb=WIDE_TILE_VMEM_MB,
             name="in_proj_uvg")
    yn, cm, states = mixers(z, xbc, dtp, uvg)
    tn = 512
    merged = mm([yn, cm], [Dot(0, w["w_br_a"]), Dot(1, w["w_br_b"])],
                [(uvg, 2 * D_MODEL // tn), (uvg, 3 * D_MODEL // tn)], _epi_gate_merge,
                n=D_MODEL, tm=512, tn=tn, out_dtype=BF16, name="branch_merge")
    x, hn, ss = mm([merged], [Dot(0, w["w_out"])], [(x, 0)], _epi_residual, next_gain=gains["g_ffn"],
                   n=D_MODEL, tm=tm, tn=1024, out_dtype=F32, name="out_proj")
    act = mm([hn], [Dot(0, w["w_gate_up"], 0, scaled=True), Dot(0, w["w_gate_up"], FFN_HIDDEN // tn, scaled=True)],
             [], _epi_swiglu, ss=ss, n=FFN_HIDDEN, tm=2 * tm, tn=tn, out_dtype=BF16, name="ffn_up")
    x, hp, ss = mm([act], [Dot(0, w["w_down"])], [(x, 0)], _epi_residual, next_gain=gains["g_ple"],
                   n=D_MODEL, tm=512, tn=tn, out_dtype=F32, name="ffn_down")
    ple_dots = [Dot(0, w["w_ple_gate"], 0, scaled=True), Dot(1, w["w_ple"])]
    if gains["g_next"] is None:
        x = mm([hp, ple_all], ple_dots, [(x, 0)], _epi_ple, ss=ss,
               n=D_MODEL, tm=512, tn=1024, out_dtype=F32, name="ple")
        return x, None, None, states
    x, hg, ss = mm([hp, ple_all], ple_dots, [(x, 0)], _epi_ple, ss=ss, next_gain=gains["g_next"],
                   n=D_MODEL, tm=512, tn=1024, out_dtype=F32, name="ple")
    return x, hg, ss, states


def kernel(x_prompt, x_sample, p_prompt, p_sample, state_ssd, state_conv, g_mix, w_in, conv_w, conv_b,
           dt_bias, a_log, d_skip, g_ssd, w_br_a, g_v, w_s, b_s, w_br_b, w_out, g_ffn, w_gate_up, w_down,
           g_ple, w_ple_gate, w_ple, g_final):
    nbp, seq, _ = x_prompt.shape
    nbs, steps, _ = x_sample.shape
    assert seq % (SSD_CHUNK * CHUNKS_PER_STEP) == 0 and seq % (CMLP_CHUNK * CMLP_CHUNKS_PER_STEP) == 0
    assert steps <= SSD_CHUNK

    head_rows = lax.broadcasted_iota(jnp.int32, (LANES, SSD_INNER), 0)
    head_cols = lax.broadcasted_iota(jnp.int32, (LANES, SSD_INNER), 1) // SSD_HEADDIM
    expand = (head_rows == head_cols).astype(BF16)
    e2, e3 = jnp.tile(expand, (2, 1)), jnp.tile(expand, (3, 1))
    pad_heads = lambda v: jnp.pad(v, (0, LANES - SSD_HEADS)).reshape(1, LANES)
    w_in_t = jnp.transpose(w_in, (0, 2, 1))
    w = dict(w_in_t=w_in_t, w_br_a=w_br_a, w_br_b=w_br_b, w_out=w_out,
             w_gate_up=w_gate_up, w_down=w_down, w_ple_gate=w_ple_gate, w_ple=w_ple)

    xp = x_prompt.reshape(nbp * seq, D_MODEL)
    xs = jnp.transpose(x_sample, (1, 0, 2)).reshape(steps * nbs, D_MODEL)
    ple_p = p_prompt.reshape(DEPTH, nbp * seq, PLE_DIM)
    ple_s = jnp.transpose(p_sample, (0, 2, 1, 3)).reshape(DEPTH, steps * nbs, PLE_DIM)
    conv_state_t = jnp.transpose(state_conv, (0, 2, 1, 3))
    h_all = state_ssd.reshape(DEPTH, nbs, SSD_INNER, SSD_STATE)
    new_states = None
    hgp, ssp = _prenorm(xp, g_mix[0])
    hgs, sss = _prenorm(xs, g_mix[0])
    outs = {k: [] for k in ("ssd_p", "conv_p", "v_p", "conv_s", "v_s")}
    for i in range(DEPTH):
        gains = dict(g_ffn=g_ffn[i], g_ple=g_ple[i], g_next=g_mix[i + 1] if i + 1 < DEPTH else None)
        bias, alog = pad_heads(dt_bias[i]), pad_heads(a_log[i])
        dsk_w = jnp.repeat(d_skip[i], SSD_HEADDIM).reshape(1, SSD_INNER)
        gs = g_ssd[i].reshape(1, SSD_INNER)

        def prompt_mixers(z, xbc, dtp, uvg, i=i, bias=bias, alog=alog, dsk_w=dsk_w, gs=gs):
            yn, st = _ssd_prompt(z, xbc, dtp, conv_w, conv_b[i], i, bias, alog, dsk_w, gs, e2, nbp, seq)
            cm, vst = _cmlp_prompt(uvg, w_s, b_s[i], g_v[i], i, nbp, seq)
            conv_new = xbc.reshape(nbp, seq, CONV_DIM)[:, seq - (CONV_W - 1):, :]
            return yn, cm, (st.reshape(nbp, SSD_HEADS, SSD_HEADDIM, SSD_STATE), conv_new, vst)

        def sample_mixers(z, xbc, dtp, uvg, i=i, bias=bias, alog=alog, dsk_w=dsk_w, gs=gs, new_states=new_states):
            xbc_t = xbc.reshape(steps, nbs, CONV_DIM)
            xc_t = _conv_sample(xbc_t, conv_state_t, conv_w, conv_b[i], i)
            seq_major = lambda t, wd: jnp.transpose(t.reshape(steps, nbs, wd), (1, 0, 2)).reshape(nbs * steps, wd)
            yn_b, st = _ssd_sample(seq_major(xc_t, CONV_DIM), seq_major(z, OFF_Z),
                                   seq_major(dtp, LANES), h_all, i, new_states,
                                   bias, alog, dsk_w, gs, e3, steps)
            yn = jnp.transpose(yn_b.reshape(nbs, steps, SSD_INNER), (1, 0, 2)).reshape(steps * nbs, SSD_INNER)
            cm_t, vn_t = _cmlp_sample(uvg.reshape(steps, nbs, 4 * D_MODEL), w_s[i], b_s[i], g_v[i])
            conv_new = jnp.transpose(xbc_t[steps - (CONV_W - 1):], (1, 0, 2))
            return (yn, cm_t.reshape(steps * nbs, CMLP_WIDTH),
                    (st, conv_new, jnp.transpose(vn_t, (1, 0, 2))))

        xp, hgp, ssp, (hp_, cp_, vp_) = _layer(xp, hgp, ssp, ple_p, w, gains, i, prompt_mixers, True)
        xs, hgs, sss, (new_states, cs_, vs_) = _layer(xs, hgs, sss, ple_s, w, gains, i, sample_mixers, False)
        outs["ssd_p"].append(hp_); outs["conv_p"].append(cp_); outs["v_p"].append(vp_)
        outs["conv_s"].append(cs_); outs["v_s"].append(vs_)

    y_prompt = _rownorm(xp, g_final, F32).reshape(nbp, seq, D_MODEL)
    y_sample = jnp.transpose(_rownorm(xs, g_final, F32).reshape(steps, nbs, D_MODEL), (1, 0, 2))
    ssd_s = new_states.reshape(DEPTH, nbs, SSD_HEADS, SSD_HEADDIM, SSD_STATE)
    return (y_prompt, y_sample, jnp.stack(outs["ssd_p"]), jnp.stack(outs["conv_p"]), jnp.stack(outs["v_p"]),
            ssd_s, jnp.stack(outs["conv_s"]), jnp.stack(outs["v_s"]))
```

```python
import functools
from typing import Any, NamedTuple

import jax
import jax.numpy as jnp
from jax import lax
from jax.experimental import pallas as pl
from jax.experimental.pallas import tpu as pltpu

F32 = jnp.float32
BF16 = jnp.bfloat16

D_MODEL = 2048
DEPTH = 4
PLE_DIM = 256
RMS_EPS = 1e-6
SSD_INNER = 4096
SSD_HEADDIM = 64
SSD_HEADS = 64
SSD_GROUPS = 8
SSD_HPG = 8
SSD_STATE = 128
SSD_CHUNK = 128
CONV_W = 4
CONV_DIM = 6144
BC_COLS = 2 * SSD_GROUPS * SSD_STATE
CMLP_WIDTH = 2048
CMLP_GROUPS = 16
CMLP_GROUP_DIM = 128
CMLP_CHUNK = 128
FFN_HIDDEN = 5632
OFF_Z = SSD_INNER
OFF_XBC = OFF_Z + CONV_DIM
OFF_DT = OFF_XBC + SSD_HEADS
IN_WIDTH = OFF_DT + 4 * D_MODEL
GROUP_COLS = SSD_HPG * SSD_HEADDIM
LANES = 128
SUBLANES = 8
HALF_LANES = LANES // 2
VMEM_LIMIT_MB = 58
WIDE_TILE_VMEM_MB = 62
NT_DIMS = (((1,), (1,)), ((), ()))


def _params(n_axes, vmem_mb=VMEM_LIMIT_MB):
    return pltpu.CompilerParams(dimension_semantics=("arbitrary",) * n_axes,
                                vmem_limit_bytes=vmem_mb * 2 ** 20)


def _softplus(x):
    return jnp.maximum(x, 0.0) + jnp.log1p(jnp.exp(-jnp.abs(x)))


def _split(q, parts):
    out, rem = [], q
    for _ in range(parts):
        piece = rem.astype(BF16)
        out.append(piece)
        rem = rem - piece.astype(F32)
    return jnp.concatenate(out, axis=1)


def _inv_rms(ss_parts, width):
    return lax.rsqrt(jnp.sum(ss_parts, axis=0) / width + RMS_EPS)


def _rownorm_body(x_ref, g_ref, o_ref):
    xf = x_ref[...]
    r = lax.rsqrt(jnp.mean(xf * xf, axis=-1, keepdims=True) + RMS_EPS)
    o_ref[...] = ((xf * r) * g_ref[...]).astype(o_ref.dtype)


def _rownorm(x, g, out_dtype, tr=512):
    m, d = x.shape
    return pl.pallas_call(
        _rownorm_body,
        grid=(m // tr,),
        in_specs=[pl.BlockSpec((tr, d), lambda i: (i, 0)),
                  pl.BlockSpec((1, d), lambda i: (0, 0))],
        out_specs=pl.BlockSpec((tr, d), lambda i: (i, 0)),
        out_shape=jax.ShapeDtypeStruct((m, d), out_dtype),
        compiler_params=_params(1, 32),
        name="rownorm",
    )(x, g.reshape(1, d))


def _prenorm_body(x_ref, g_ref, xg_ref, ss_ref):
    xf = x_ref[...]
    xg_ref[...] = (xf * g_ref[...]).astype(xg_ref.dtype)
    ss_ref[...] = jnp.sum(xf * xf, axis=-1, keepdims=True)


def _prenorm(x, g, tr=512):
    m, d = x.shape
    return pl.pallas_call(
        _prenorm_body,
        grid=(m // tr,),
        in_specs=[pl.BlockSpec((tr, d), lambda i: (i, 0)),
                  pl.BlockSpec((1, d), lambda i: (0, 0))],
        out_specs=[pl.BlockSpec((tr, d), lambda i: (i, 0)),
                   pl.BlockSpec((None, tr, 1), lambda i: (0, i, 0))],
        out_shape=[jax.ShapeDtypeStruct((m, d), BF16), jax.ShapeDtypeStruct((1, m, 1), F32)],
        compiler_params=_params(1, 32),
        name="prenorm",
    )(x, g.reshape(1, d))


class Dot(NamedTuple):
    act: int
    w: Any
    off: int = 0
    scaled: bool = False
    transposed: bool = False
    shifted: bool = False


def _fused_mm_body(*refs, n_acts, dots, n_extras, epi, has_ss, has_gain):
    a_refs = refs[:n_acts]
    pos = n_acts
    w_refs = []
    for d in dots:
        w_refs.append(refs[pos:pos + (2 if d.shifted else 1)])
        pos += 2 if d.shifted else 1
    ss_ref = refs[pos] if has_ss else None
    pos += has_ss
    e_refs = refs[pos:pos + n_extras]
    pos += n_extras
    gain_ref = refs[pos] if has_gain else None
    pos += has_gain
    o_ref = refs[pos]
    pos += 1
    if has_gain:
        xg_ref, ssq_ref = refs[pos:pos + 2]
        pos += 2
    scr = refs[pos:]

    if scr:
        @pl.when(pl.program_id(1) == 0)
        def _():
            for d, wr, s in zip(dots, w_refs, scr):
                if d.shifted:
                    keep = s.shape[0] - HALF_LANES
                    s[:keep, :] = wr[0][HALF_LANES:, :].astype(BF16)
                    s[keep:, :] = wr[1][...].astype(BF16)
                else:
                    s[...] = wr[0][...].astype(BF16)
        w_vals = [s[...] for s in scr]
    else:
        w_vals = [jnp.concatenate([wr[0][HALF_LANES:, :], wr[1][...]], axis=0).astype(BF16) if d.shifted
                  else wr[0][...].astype(BF16) for d, wr in zip(dots, w_refs)]

    acts = [a[...].astype(BF16) for a in a_refs]
    r = _inv_rms(ss_ref[...], D_MODEL) if has_ss else None
    accs = []
    for d, wv in zip(dots, w_vals):
        if d.transposed:
            acc = lax.dot_general(acts[d.act], wv, NT_DIMS, preferred_element_type=F32)
        else:
            acc = jnp.dot(acts[d.act], wv, preferred_element_type=F32)
        accs.append(acc * r if d.scaled else acc)
    out = epi(accs, [e[...] for e in e_refs])
    o_ref[...] = out.astype(o_ref.dtype)
    if has_gain:
        xg_ref[...] = (out * gain_ref[...]).astype(xg_ref.dtype)
        ssq_ref[...] = jnp.sum(out * out, axis=-1, keepdims=True)


def _fused_mm(acts, dots, extras, epi, *, layer, n, tm, tn, out_dtype, name, ss=None, next_gain=None,
              vmem_mb=VMEM_LIMIT_MB):
    m = acts[0].shape[-2]
    tm = min(tm, m)
    in_specs, args, scratch = [], [], []
    for a in acts:
        if a.ndim == 3:
            in_specs.append(pl.BlockSpec((None, tm, a.shape[2]), lambda j, i: (layer, i, 0)))
        else:
            in_specs.append(pl.BlockSpec((tm, a.shape[1]), lambda j, i: (i, 0)))
        args.append(a)
    for d in dots:
        k = acts[d.act].shape[-1]
        if d.transposed:
            in_specs.append(pl.BlockSpec((None, tn, k), lambda j, i, off=d.off: (layer, j + off, 0)))
            args.append(d.w)
            if d.shifted:
                per = tn // HALF_LANES
                in_specs.append(pl.BlockSpec((None, HALF_LANES, k),
                                             lambda j, i, off=d.off, per=per: (layer, (j + off + 1) * per, 0)))
                args.append(d.w)
            scratch.append(pltpu.VMEM((tn, k), BF16))
        else:
            in_specs.append(pl.BlockSpec((None, k, tn), lambda j, i, off=d.off: (layer, 0, j + off)))
            args.append(d.w)
            scratch.append(pltpu.VMEM((k, tn), BF16))
    if ss is not None:
        in_specs.append(pl.BlockSpec((ss.shape[0], tm, 1), lambda j, i: (0, i, 0)))
        args.append(ss)
    for e, off in extras:
        in_specs.append(pl.BlockSpec((tm, tn), lambda j, i, off=off: (i, j + off)))
        args.append(e)
    out_specs = [pl.BlockSpec((tm, tn), lambda j, i: (i, j))]
    out_shape = [jax.ShapeDtypeStruct((m, n), out_dtype)]
    if next_gain is not None:
        in_specs.append(pl.BlockSpec((1, tn), lambda j, i: (0, j)))
        args.append(next_gain.reshape(1, n))
        out_specs += [pl.BlockSpec((tm, tn), lambda j, i: (i, j)),
                      pl.BlockSpec((None, tm, 1), lambda j, i: (j, i, 0))]
        out_shape += [jax.ShapeDtypeStruct((m, n), BF16), jax.ShapeDtypeStruct((n // tn, m, 1), F32)]
    body = functools.partial(_fused_mm_body, n_acts=len(acts), dots=tuple(d._replace(w=None) for d in dots),
                             n_extras=len(extras), epi=epi, has_ss=ss is not None,
                             has_gain=next_gain is not None)
    res = pl.pallas_call(
        body,
        grid=(n // tn, m // tm),
        in_specs=in_specs,
        out_specs=out_specs,
        out_shape=out_shape,
        scratch_shapes=scratch if m > tm else [],
        compiler_params=_params(2, vmem_mb),
        name=name,
    )(*args)
    return res if next_gain is not None else res[0]


def _epi_first(accs, extras):
    return accs[0]


def _epi_residual(accs, extras):
    return extras[0] + accs[0]


def _epi_gate_merge(accs, extras):
    return (jax.nn.sigmoid(extras[0].astype(F32)) * accs[0]
            + jax.nn.sigmoid(extras[1].astype(F32)) * accs[1])


def _epi_swiglu(accs, extras):
    return jax.nn.silu(accs[0]) * accs[1]


def _epi_ple(accs, extras):
    return extras[0] + jax.nn.sigmoid(accs[0]) * accs[1]


def _conv_silu_rows(raw_ref, prev_rows, w_ref, b_ref):
    n_rows = raw_ref.shape[0]
    head = raw_ref[:SUBLANES, :]
    ext = jnp.concatenate([prev_rows, head], axis=0)
    acc_head = b_ref[...] + w_ref[CONV_W - 1:CONV_W, :] * head
    acc_rest = b_ref[...] + w_ref[CONV_W - 1:CONV_W, :] * raw_ref[SUBLANES:, :]
    for j in range(1, CONV_W):
        wj = w_ref[CONV_W - 1 - j:CONV_W - j, :]
        acc_head = acc_head + wj * pltpu.roll(ext, j, axis=0)[SUBLANES:, :]
        acc_rest = acc_rest + wj * raw_ref[pl.ds(SUBLANES - j, n_rows - SUBLANES), :]
    acc = jnp.concatenate([acc_head, acc_rest], axis=0)
    return acc * jax.nn.sigmoid(acc)


def _conv_sample_body(x_ref, s_ref, w_ref, b_ref, o_ref):
    steps = x_ref.shape[0]
    xpad = [s_ref[k] for k in range(CONV_W - 1)] + [x_ref[t] for t in range(steps)]
    for t in range(steps):
        acc = b_ref[...] + w_ref[0:1, :] * xpad[t]
        for k in range(1, CONV_W):
            acc = acc + w_ref[k:k + 1, :] * xpad[t + k]
        o_ref[t] = acc * jax.nn.sigmoid(acc)


def _conv_sample(xbc_t, state_t, conv_w, conv_b, layer, tc=1024):
    steps, nb, _ = xbc_t.shape
    return pl.pallas_call(
        _conv_sample_body,
        grid=(CONV_DIM // tc,),
        in_specs=[pl.BlockSpec((steps, nb, tc), lambda c: (0, 0, c)),
                  pl.BlockSpec((None, CONV_W - 1, nb, tc), lambda c: (layer, 0, 0, c)),
                  pl.BlockSpec((None, CONV_W, tc), lambda c: (layer, 0, c)),
                  pl.BlockSpec((1, tc), lambda c: (0, c))],
        out_specs=pl.BlockSpec((steps, nb, tc), lambda c: (0, 0, c)),
        out_shape=jax.ShapeDtypeStruct((steps, nb, CONV_DIM), F32),
        compiler_params=_params(1, 32),
        name="conv_sample",
    )(xbc_t, state_t, conv_w, conv_b.reshape(1, CONV_DIM))


HEADS_PER_DOT = 4
QUAD_COLS = HEADS_PER_DOT * SSD_HEADDIM
CHUNKS_PER_STEP = 2


def _gated_norm(y2, g):
    r = lax.rsqrt(jnp.mean(y2 * y2, axis=-1, keepdims=True) + RMS_EPS)
    return (y2 * r) * g


def _ssd_prompt_body(z_ref, xr_ref, bcr_ref, dtp_ref, cwx_ref, cwb_ref, cbx_ref, cbb_ref,
                     bias_ref, alog_ref, dsk_ref, g_ref, e2_ref,
                     yn_ref, st_ref, s_scr, y_scr, prev_x, prev_bc, *, n_steps):
    c = pl.program_id(1)
    q = SSD_CHUNK
    rows_n = q * CHUNKS_PER_STEP

    @pl.when(c == 0)
    def _():
        s_scr[...] = jnp.zeros_like(s_scr)
        prev_x[...] = jnp.zeros_like(prev_x)
        prev_bc[...] = jnp.zeros_like(prev_bc)

    xs_all = _conv_silu_rows(xr_ref, prev_x[...], cwx_ref, cbx_ref)
    bc_all = _conv_silu_rows(bcr_ref, prev_bc[...], cwb_ref, cbb_ref)
    prev_x[...] = xr_ref[rows_n - SUBLANES:, :]
    prev_bc[...] = bcr_ref[rows_n - SUBLANES:, :]

    lane = lax.broadcasted_iota(jnp.int32, (q, LANES), 1)
    row = lax.broadcasted_iota(jnp.int32, (q, LANES), 0)
    head_ok = lane < SSD_HEADS
    causal = row >= lane
    tri = causal.astype(BF16)
    lane_q = lax.broadcasted_iota(jnp.int32, (q, QUAD_COLS), 1) // SSD_HEADDIM
    head_masks = [(lane_q == k).astype(BF16) for k in range(HEADS_PER_DOT)]
    a = -jnp.exp(alog_ref[...])

    for sub in range(CHUNKS_PER_STEP):
        rs = slice(sub * q, (sub + 1) * q)
        dt = jnp.where(head_ok, _softplus(dtp_ref[rs, :] + bias_ref[...]), 0.0)
        dta = dt * a
        cum3 = jnp.dot(tri, _split(dta, 3), preferred_element_type=F32)
        cum = cum3[:, :LANES] + cum3[:, LANES:2 * LANES] + cum3[:, 2 * LANES:]
        cum_t = (cum - jnp.where(head_ok, jnp.log(dt), 0.0)).T
        last = cum[q - 1:q, :]
        ecum = jnp.exp(cum)
        dtw = jnp.where(head_ok, dt * jnp.exp(last - cum), 0.0)
        stack = jnp.concatenate([dtw, ecum], axis=0)
        wide = jnp.dot(_split(stack, 2), e2_ref[...], preferred_element_type=F32)
        dtw_w, ecum_w = wide[:q], wide[q:]
        cdec_w = ecum_w[q - 1:q, :]

        xs = xs_all[rs]
        xs_b = xs.astype(BF16)
        xdd_b = (xs * dtw_w).astype(BF16)
        for g in range(SSD_GROUPS):
            gs = slice(g * GROUP_COLS, (g + 1) * GROUP_COLS)
            b_g = bc_all[rs, g * SSD_STATE:(g + 1) * SSD_STATE]
            c_g = bc_all[rs, (SSD_GROUPS + g) * SSD_STATE:(SSD_GROUPS + g + 1) * SSD_STATE]
            c_b = c_g.astype(BF16)
            cb = lax.dot_general(c_b, b_g.astype(BF16), NT_DIMS, preferred_element_type=F32)
            s_g = s_scr[g]
            y_off = jnp.dot(c_b, s_g.astype(BF16), preferred_element_type=F32)
            y_quads = []
            for quad in range(SSD_HPG // HEADS_PER_DOT):
                h0 = g * SSD_HPG + quad * HEADS_PER_DOT
                m_heads = []
                for k in range(HEADS_PER_DOT):
                    h = h0 + k
                    diff = cum[:, h:h + 1] - cum_t[h:h + 1, :]
                    seg = jnp.exp(jnp.where(causal, diff, -jnp.inf))
                    m_heads.append((cb * seg).astype(BF16))
                xq = xs_b[:, h0 * SSD_HEADDIM:h0 * SSD_HEADDIM + QUAD_COLS]
                rhs = jnp.concatenate([xq * head_masks[k] for k in range(HEADS_PER_DOT)], axis=0)
                y_quads.append(jnp.dot(jnp.concatenate(m_heads, axis=1), rhs, preferred_element_type=F32))
            y_g = jnp.concatenate(y_quads, axis=1) + y_off * ecum_w[:, gs]
            upd = jnp.dot(b_g.T.astype(BF16), xdd_b[:, gs], preferred_element_type=F32)
            s_new = s_g * cdec_w[:, gs] + upd
            s_scr[g] = s_new

            if sub == CHUNKS_PER_STEP - 1:
                @pl.when(c == n_steps - 1)
                def _(s_new=s_new, gs=gs):
                    st_ref[0, gs, :] = s_new.T

            zg = z_ref[rs, gs].astype(F32)
            y_scr[rs, gs] = (y_g + xs[:, gs] * dsk_ref[:, gs]) * (zg * jax.nn.sigmoid(zg))
    yn_ref[...] = _gated_norm(y_scr[...], g_ref[...]).astype(yn_ref.dtype)


def _ssd_prompt(z, xbc, dtp, conv_w, conv_b, layer, bias, alog, dsk_w, g_ssd, e2, nb, seq):
    rows_n = SSD_CHUNK * CHUNKS_PER_STEP
    ns = seq // rows_n
    body = functools.partial(_ssd_prompt_body, n_steps=ns)
    const = lambda b, c: (0, 0)
    rows = lambda b, c: (b * ns + c, 0)
    cb2 = conv_b.reshape(1, CONV_DIM)
    return pl.pallas_call(
        body,
        grid=(nb, ns),
        in_specs=[pl.BlockSpec((rows_n, SSD_INNER), rows),
                  pl.BlockSpec((rows_n, SSD_INNER), rows),
                  pl.BlockSpec((rows_n, BC_COLS), lambda b, c: (b * ns + c, SSD_INNER // BC_COLS)),
                  pl.BlockSpec((rows_n, LANES), rows),
                  pl.BlockSpec((None, CONV_W, SSD_INNER), lambda b, c: (layer, 0, 0)),
                  pl.BlockSpec((None, CONV_W, BC_COLS), lambda b, c: (layer, 0, SSD_INNER // BC_COLS)),
                  pl.BlockSpec((1, SSD_INNER), const),
                  pl.BlockSpec((1, BC_COLS), lambda b, c: (0, SSD_INNER // BC_COLS)),
                  pl.BlockSpec((1, LANES), const),
                  pl.BlockSpec((1, LANES), const),
                  pl.BlockSpec((1, SSD_INNER), const),
                  pl.BlockSpec((1, SSD_INNER), const),
                  pl.BlockSpec((2 * LANES, SSD_INNER), const)],
        out_specs=[pl.BlockSpec((rows_n, SSD_INNER), rows),
                   pl.BlockSpec((1, SSD_INNER, SSD_STATE), lambda b, c: (b, 0, 0))],
        out_shape=[jax.ShapeDtypeStruct((nb * seq, SSD_INNER), BF16),
                   jax.ShapeDtypeStruct((nb, SSD_INNER, SSD_STATE), F32)],
        scratch_shapes=[pltpu.VMEM((SSD_GROUPS, SSD_STATE, GROUP_COLS), F32),
                        pltpu.VMEM((rows_n, SSD_INNER), F32),
                        pltpu.VMEM((SUBLANES, SSD_INNER), F32),
                        pltpu.VMEM((SUBLANES, BC_COLS), F32)],
        compiler_params=_params(2, 56),
        name="ssd_prompt",
    )(z, xbc, xbc, dtp, conv_w, conv_w, cb2, cb2, bias, alog, dsk_w, g_ssd, e2)


SEQ_PER_STEP = 4


def _ssd_sample_body(xc_ref, z_ref, dtp_ref, h0_ref, bias_ref, alog_ref, dsk_ref, g_ref, e3_ref, *rest, steps):
    yn_ref, hn_ref = rest[-2:]
    rows_n = SEQ_PER_STEP * steps
    lane = lax.broadcasted_iota(jnp.int32, (rows_n, LANES), 1)
    row = lax.broadcasted_iota(jnp.int32, (rows_n, LANES), 0)
    tpos = row % steps
    head_ok = lane < SSD_HEADS

    def shift(v, j, tp):
        return jnp.where(tp >= j, pltpu.roll(v, j, axis=0), 0.0)

    dt = jnp.where(head_ok, _softplus(dtp_ref[...] + bias_ref[...]), 0.0)
    a = -jnp.exp(alog_ref[...])
    dta = dt * a
    cum = dta
    for j in range(1, steps):
        cum = cum + shift(dta, j, tpos)
    last = jnp.zeros_like(cum)
    for k in range(steps):
        back = steps - 1 - k
        src = cum if back == 0 else pltpu.roll(cum, rows_n - back, axis=0)
        last = jnp.where(tpos == k, src, last)
    ecum = jnp.exp(cum)
    dte = jnp.where(head_ok, jnp.exp(last - cum), 0.0)
    cdec = jnp.exp(last)

    b_all = xc_ref[:, SSD_INNER:SSD_INNER + SSD_GROUPS * SSD_STATE]
    c_all = xc_ref[:, SSD_INNER + SSD_GROUPS * SSD_STATE:]
    row_w = lax.broadcasted_iota(jnp.int32, b_all.shape, 0)
    tpos_w = row_w % steps
    coefs = []
    for j in range(steps):
        prod = c_all * (b_all if j == 0 else shift(b_all, j, tpos_w))
        cb = jnp.zeros((rows_n, LANES), F32)
        for g in range(SSD_GROUPS):
            cbg = jnp.sum(prod[:, g * SSD_STATE:(g + 1) * SSD_STATE], axis=1, keepdims=True)
            cb = jnp.where((lane // SSD_HPG) == g, cbg, cb)
        if j == 0:
            coefs.append(jnp.where(head_ok, cb, 0.0))
        else:
            seg = jnp.exp(cum - shift(cum, j, tpos))
            coefs.append(jnp.where(head_ok & (tpos >= j), cb * seg, 0.0))
    stack = jnp.concatenate([dt, dte, ecum, cdec] + coefs, axis=0)
    wide = jnp.dot(_split(stack, 3), e3_ref[...], preferred_element_type=F32)
    dt_w, dte_w = wide[:rows_n], wide[rows_n:2 * rows_n]
    ecum_w, cdec_w = wide[2 * rows_n:3 * rows_n], wide[3 * rows_n:4 * rows_n]

    xs = xc_ref[:, :SSD_INNER]
    xdt = xs * dt_w
    y = wide[4 * rows_n:5 * rows_n] * xdt
    for j in range(1, steps):
        y = y + wide[(4 + j) * rows_n:(5 + j) * rows_n] * pltpu.roll(xdt, j, axis=0)
    xdd = xdt * dte_w

    row_g = lax.broadcasted_iota(jnp.int32, (rows_n, GROUP_COLS), 0)
    ones_b = jnp.ones((rows_n, SSD_STATE), BF16)
    c_bf = c_all.astype(BF16)
    b_bf = b_all.astype(BF16)
    y_off_groups = []
    for g in range(SSD_GROUPS):
        gs = slice(g * GROUP_COLS, (g + 1) * GROUP_COLS)
        ns = slice(g * SSD_STATE, (g + 1) * SSD_STATE)
        y_off = jnp.zeros((rows_n, GROUP_COLS), F32)
        for i in range(SEQ_PER_STEP):
            mine = (row_g // steps) == i
            h0g = h0_ref[i, gs, :]
            yo = lax.dot_general(c_bf[:, ns], h0g.astype(BF16), NT_DIMS, preferred_element_type=F32)
            y_off = jnp.where(mine, yo, y_off)
            x_i = jnp.where(mine, xdd[:, gs], 0.0).astype(BF16)
            upd = lax.dot_general(x_i, b_bf[:, ns], (((0,), (0,)), ((), ())),
                                  preferred_element_type=F32)
            cd = cdec_w[i * steps:i * steps + 1, gs]
            cd_hi = cd.astype(BF16).astype(F32)
            cd_mid = (cd - cd_hi).astype(BF16).astype(F32)
            cd_lo = cd - cd_hi - cd_mid
            cd_rows = jnp.where(row_g == 0, cd_hi,
                                jnp.where(row_g == 1, cd_mid,
                                          jnp.where(row_g == 2, cd_lo, 0.0))).astype(BF16)
            cd_col = lax.dot_general(cd_rows, ones_b, (((0,), (0,)), ((), ())),
                                     preferred_element_type=F32)
            hn_ref[i, gs, :] = h0g * cd_col + upd
        y_off_groups.append(y_off)
    y = y + jnp.concatenate(y_off_groups, axis=1) * ecum_w
    zz = z_ref[...].astype(F32)
    y2 = (y + xs * dsk_ref[...]) * (zz * jax.nn.sigmoid(zz))
    yn_ref[...] = _gated_norm(y2, g_ref[...]).astype(yn_ref.dtype)


def _ssd_sample(xc, z, dtp, h_all, layer, new_states, bias, alog, dsk_w, g_ssd, e3, steps):
    nb = h_all.shape[1]
    rows_n = SEQ_PER_STEP * steps
    body = functools.partial(_ssd_sample_body, steps=steps)
    const = lambda i: (0, 0)
    state_spec = pl.BlockSpec((None, SEQ_PER_STEP, SSD_INNER, SSD_STATE), lambda i: (layer, i, 0, 0))
    in_specs = [pl.BlockSpec((rows_n, CONV_DIM), lambda i: (i, 0)),
                pl.BlockSpec((rows_n, SSD_INNER), lambda i: (i, 0)),
                pl.BlockSpec((rows_n, LANES), lambda i: (i, 0)),
                state_spec,
                pl.BlockSpec((1, LANES), const),
                pl.BlockSpec((1, LANES), const),
                pl.BlockSpec((1, SSD_INNER), const),
                pl.BlockSpec((1, SSD_INNER), const),
                pl.BlockSpec((3 * LANES, SSD_INNER), const)]
    args = [xc, z, dtp, h_all, bias, alog, dsk_w, g_ssd, e3]
    aliases = {}
    if new_states is not None:
        in_specs.append(pl.BlockSpec(memory_space=pl.ANY))
        args.append(new_states)
        aliases = {len(args) - 1: 1}
    return pl.pallas_call(
        body,
        grid=(nb // SEQ_PER_STEP,),
        in_specs=in_specs,
        out_specs=[pl.BlockSpec((rows_n, SSD_INNER), lambda i: (i, 0)), state_spec],
        out_shape=[jax.ShapeDtypeStruct((nb * steps, SSD_INNER), BF16),
                   jax.ShapeDtypeStruct(h_all.shape, F32)],
        input_output_aliases=aliases,
        compiler_params=_params(1, 56),
        name="ssd_sample",
    )(*args)


def _vnorm(v, g):
    v = jax.nn.gelu(v)
    r = lax.rsqrt(jnp.mean(v * v, axis=-1, keepdims=True) + RMS_EPS)
    return (v * r) * g


CMLP_CHUNKS_PER_STEP = 4


def _cmlp_prompt_body(u_ref, v_ref, ws_ref, bst_ref, gv_ref, cm_ref, vs_ref, *, n_steps):
    q = CMLP_CHUNK
    row = lax.broadcasted_iota(jnp.int32, (q, q), 0)
    col = lax.broadcasted_iota(jnp.int32, (q, q), 1)
    wms = [jnp.where(row >= col, ws_ref[g], 0.0).astype(BF16) for g in range(CMLP_GROUPS)]
    for sub in range(CMLP_CHUNKS_PER_STEP):
        rs = slice(sub * q, (sub + 1) * q)
        vn = _vnorm(v_ref[rs, :].astype(F32), gv_ref[...])
        if sub == CMLP_CHUNKS_PER_STEP - 1:
            @pl.when(pl.program_id(0) % n_steps == n_steps - 1)
            def _(vn=vn):
                vs_ref[0] = vn

        u = jax.nn.gelu(u_ref[rs, :].astype(F32))
        vb = vn.astype(BF16)
        for g in range(CMLP_GROUPS):
            gs = slice(g * CMLP_GROUP_DIM, (g + 1) * CMLP_GROUP_DIM)
            mixed = jnp.dot(wms[g], vb[:, gs], preferred_element_type=F32) + bst_ref[:, g:g + 1]
            cm_ref[rs, gs] = (u[:, gs] * mixed).astype(cm_ref.dtype)


def _cmlp_prompt(uvg, w_s, b_s, g_v, layer, nb, seq):
    q = CMLP_CHUNK
    rows_n = q * CMLP_CHUNKS_PER_STEP
    nc = seq // rows_n
    body = functools.partial(_cmlp_prompt_body, n_steps=nc)
    return pl.pallas_call(
        body,
        grid=(nb * nc,),
        in_specs=[pl.BlockSpec((rows_n, CMLP_WIDTH), lambda i: (i, 0)),
                  pl.BlockSpec((rows_n, CMLP_WIDTH), lambda i: (i, 1)),
                  pl.BlockSpec((None, CMLP_GROUPS, q, q), lambda i: (layer, 0, 0, 0)),
                  pl.BlockSpec((q, CMLP_GROUPS), lambda i: (0, 0)),
                  pl.BlockSpec((1, CMLP_WIDTH), lambda i: (0, 0))],
        out_specs=[pl.BlockSpec((rows_n, CMLP_WIDTH), lambda i: (i, 0)),
                   pl.BlockSpec((1, q, CMLP_WIDTH), lambda i: (i // nc, 0, 0))],
        out_shape=[jax.ShapeDtypeStruct((nb * seq, CMLP_WIDTH), BF16),
                   jax.ShapeDtypeStruct((nb, q, CMLP_WIDTH), F32)],
        compiler_params=_params(1, 32),
        name="cmlp_prompt",
    )(uvg, uvg, w_s, b_s.T, g_v.reshape(1, CMLP_WIDTH))


def _cmlp_sample_body(u_ref, v_ref, wrow_ref, brow_ref, gv_ref, cm_ref, vs_ref):
    steps = u_ref.shape[0]
    vn = [_vnorm(v_ref[t].astype(F32), gv_ref[...]) for t in range(steps)]
    for t in range(steps):
        vs_ref[t] = vn[t]
        mixed = brow_ref[t:t + 1, :] + wrow_ref[t, 0:1, :] * vn[0]
        for s in range(1, t + 1):
            mixed = mixed + wrow_ref[t, s:s + 1, :] * vn[s]
        cm_ref[t] = (jax.nn.gelu(u_ref[t].astype(F32)) * mixed).astype(cm_ref.dtype)


def _cmlp_sample(uvg_t, w_s, b_s, g_v, tb=32):
    steps, nb, _ = uvg_t.shape
    wrow = jnp.repeat(jnp.transpose(w_s[:, :steps, :steps], (1, 2, 0)), CMLP_GROUP_DIM, axis=-1)
    brow = jnp.repeat(b_s[:, :steps].T, CMLP_GROUP_DIM, axis=-1)
    return pl.pallas_call(
        _cmlp_sample_body,
        grid=(nb // tb,),
        in_specs=[pl.BlockSpec((steps, tb, CMLP_WIDTH), lambda i: (0, i, 0)),
                  pl.BlockSpec((steps, tb, CMLP_WIDTH), lambda i: (0, i, 1)),
                  pl.BlockSpec((steps, steps, CMLP_WIDTH), lambda i: (0, 0, 0)),
                  pl.BlockSpec((steps, CMLP_WIDTH), lambda i: (0, 0)),
                  pl.BlockSpec((1, CMLP_WIDTH), lambda i: (0, 0))],
        out_specs=[pl.BlockSpec((steps, tb, CMLP_WIDTH), lambda i: (0, i, 0)),
                   pl.BlockSpec((steps, tb, CMLP_WIDTH), lambda i: (0, i, 0))],
        out_shape=[jax.ShapeDtypeStruct((steps, nb, CMLP_WIDTH), BF16),
                   jax.ShapeDtypeStruct((steps, nb, CMLP_WIDTH), F32)],
        compiler_params=_params(1, 32),
        name="cmlp_sample",
    )(uvg_t, uvg_t, wrow, brow, g_v.reshape(1, CMLP_WIDTH))


def _layer(x, hg, ss, ple_all, w, gains, layer, mixers, big):
    tm = 1024 if big else 512
    mm = functools.partial(_fused_mm, layer=layer)
    w_in_t = w["w_in_t"]
    tn_in = 1024 if big else 2048
    z = mm([hg], [Dot(0, w_in_t, 0, scaled=True, transposed=True)], [], _epi_first, ss=ss,
           n=OFF_Z, tm=2 * tm, tn=tn_in, out_dtype=BF16, vmem_mb=WIDE_TILE_VMEM_MB, name="in_proj_z")
    xbc = mm([hg], [Dot(0, w_in_t, OFF_Z // tn_in, scaled=True, transposed=True)], [], _epi_first, ss=ss,
             n=CONV_DIM, tm=tm, tn=tn_in, out_dtype=F32, vmem_mb=WIDE_TILE_VMEM_MB, name="in_proj_xbc")
    dtp = mm([hg], [Dot(0, w_in_t, OFF_XBC // LANES, scaled=True, transposed=True)], [], _epi_first, ss=ss,
             n=LANES, tm=tm, tn=LANES, out_dtype=F32, name="in_proj_dt")
    uvg = mm([hg], [Dot(0, w_in_t, OFF_XBC // tn_in, scaled=True, transposed=True, shifted=True)], [],
             _epi_first, ss=ss, n=4 * D_MODEL, tm=2 * tm, tn=tn_in, out_dtype=BF16, vmem_mb=WIDE_TILE_VMEM_MB,
             name="in_proj_uvg")
    yn, cm, states = mixers(z, xbc, dtp, uvg)
    tn = 512
    merged = mm([yn, cm], [Dot(0, w["w_br_a"]), Dot(1, w["w_br_b"])],
                [(uvg, 2 * D_MODEL // tn), (uvg, 3 * D_MODEL // tn)], _epi_gate_merge,
                n=D_MODEL, tm=512, tn=tn, out_dtype=BF16, name="branch_merge")
    x, hn, ss = mm([merged], [Dot(0, w["w_out"])], [(x, 0)], _epi_residual, next_gain=gains["g_ffn"],
                   n=D_MODEL, tm=tm, tn=1024, out_dtype=F32, name="out_proj")
    act = mm([hn], [Dot(0, w["w_gate_up"], 0, scaled=True), Dot(0, w["w_gate_up"], FFN_HIDDEN // tn, scaled=True)],
             [], _epi_swiglu, ss=ss, n=FFN_HIDDEN, tm=2 * tm, tn=tn, out_dtype=BF16, name="ffn_up")
    x, hp, ss = mm([act], [Dot(0, w["w_down"])], [(x, 0)], _epi_residual, next_gain=gains["g_ple"],
                   n=D_MODEL, tm=512, tn=tn, out_dtype=F32, name="ffn_down")
    ple_dots = [Dot(0, w["w_ple_gate"], 0, scaled=True), Dot(1, w["w_ple"])]
    if gains["g_next"] is None:
        x = mm([hp, ple_all], ple_dots, [(x, 0)], _epi_ple, ss=ss,
               n=D_MODEL, tm=512, tn=1024, out_dtype=F32, name="ple")
        return x, None, None, states
    x, hg, ss = mm([hp, ple_all], ple_dots, [(x, 0)], _epi_ple, ss=ss, next_gain=gains["g_next"],
                   n=D_MODEL, tm=512, tn=1024, out_dtype=F32, name="ple")
    return x, hg, ss, states


def kernel(x_prompt, x_sample, p_prompt, p_sample, state_ssd, state_conv, g_mix, w_in, conv_w, conv_b,
           dt_bias, a_log, d_skip, g_ssd, w_br_a, g_v, w_s, b_s, w_br_b, w_out, g_ffn, w_gate_up, w_down,
           g_ple, w_ple_gate, w_ple, g_final):
    nbp, seq, _ = x_prompt.shape
    nbs, steps, _ = x_sample.shape
    assert seq % (SSD_CHUNK * CHUNKS_PER_STEP) == 0 and seq % (CMLP_CHUNK * CMLP_CHUNKS_PER_STEP) == 0
    assert steps <= SSD_CHUNK

    head_rows = lax.broadcasted_iota(jnp.int32, (LANES, SSD_INNER), 0)
    head_cols = lax.broadcasted_iota(jnp.int32, (LANES, SSD_INNER), 1) // SSD_HEADDIM
    expand = (head_rows == head_cols).astype(BF16)
    e2, e3 = jnp.tile(expand, (2, 1)), jnp.tile(expand, (3, 1))
    pad_heads = lambda v: jnp.pad(v, (0, LANES - SSD_HEADS)).reshape(1, LANES)
    w_in_t = jnp.transpose(w_in, (0, 2, 1))
    w = dict(w_in_t=w_in_t, w_br_a=w_br_a, w_br_b=w_br_b, w_out=w_out,
             w_gate_up=w_gate_up, w_down=w_down, w_ple_gate=w_ple_gate, w_ple=w_ple)

    xp = x_prompt.reshape(nbp * seq, D_MODEL)
    xs = jnp.transpose(x_sample, (1, 0, 2)).reshape(steps * nbs, D_MODEL)
    ple_p = p_prompt.reshape(DEPTH, nbp * seq, PLE_DIM)
    ple_s = jnp.transpose(p_sample, (0, 2, 1, 3)).reshape(DEPTH, steps * nbs, PLE_DIM)
    conv_state_t = jnp.transpose(state_conv, (0, 2, 1, 3))
    h_all = state_ssd.reshape(DEPTH, nbs, SSD_INNER, SSD_STATE)
    new_states = None
    hgp, ssp = _prenorm(xp, g_mix[0])
    hgs, sss = _prenorm(xs, g_mix[0])
    outs = {k: [] for k in ("ssd_p", "conv_p", "v_p", "conv_s", "v_s")}
    for i in range(DEPTH):
        gains = dict(g_ffn=g_ffn[i], g_ple=g_ple[i], g_next=g_mix[i + 1] if i + 1 < DEPTH else None)
        bias, alog = pad_heads(dt_bias[i]), pad_heads(a_log[i])
        dsk_w = jnp.repeat(d_skip[i], SSD_HEADDIM).reshape(1, SSD_INNER)
        gs = g_ssd[i].reshape(1, SSD_INNER)

        def prompt_mixers(z, xbc, dtp, uvg, i=i, bias=bias, alog=alog, dsk_w=dsk_w, gs=gs):
            yn, st = _ssd_prompt(z, xbc, dtp, conv_w, conv_b[i], i, bias, alog, dsk_w, gs, e2, nbp, seq)
            cm, vst = _cmlp_prompt(uvg, w_s, b_s[i], g_v[i], i, nbp, seq)
            conv_new = xbc.reshape(nbp, seq, CONV_DIM)[:, seq - (CONV_W - 1):, :]
            return yn, cm, (st.reshape(nbp, SSD_HEADS, SSD_HEADDIM, SSD_STATE), conv_new, vst)

        def sample_mixers(z, xbc, dtp, uvg, i=i, bias=bias, alog=alog, dsk_w=dsk_w, gs=gs, new_states=new_states):
            xbc_t = xbc.reshape(steps, nbs, CONV_DIM)
            xc_t = _conv_sample(xbc_t, conv_state_t, conv_w, conv_b[i], i)
            seq_major = lambda t, wd: jnp.transpose(t.reshape(steps, nbs, wd), (1, 0, 2)).reshape(nbs * steps, wd)
            yn_b, st = _ssd_sample(seq_major(xc_t, CONV_DIM), seq_major(z, OFF_Z),
                                   seq_major(dtp, LANES), h_all, i, new_states,
                                   bias, alog, dsk_w, gs, e3, steps)
            yn = jnp.transpose(yn_b.reshape(nbs, steps, SSD_INNER), (1, 0, 2)).reshape(steps * nbs, SSD_INNER)
            cm_t, vn_t = _cmlp_sample(uvg.reshape(steps, nbs, 4 * D_MODEL), w_s[i], b_s[i], g_v[i])
            conv_new = jnp.transpose(xbc_t[steps - (CONV_W - 1):], (1, 0, 2))
            return (yn, cm_t.reshape(steps * nbs, CMLP_WIDTH),
                    (st, conv_new, jnp.transpose(vn_t, (1, 0, 2))))

        xp, hgp, ssp, (hp_, cp_, vp_) = _layer(xp, hgp, ssp, ple_p, w, gains, i, prompt_mixers, True)
        xs, hgs, sss, (new_states, cs_, vs_) = _layer(xs, hgs, sss, ple_s, w, gains, i, sample_mixers, False)
        outs["ssd_p"].append(hp_); outs["conv_p"].append(cp_); outs["v_p"].append(vp_)
        outs["conv_s"].append(cs_); outs["v_s"].append(vs_)

    y_prompt = _rownorm(xp, g_final, F32).reshape(nbp, seq, D_MODEL)
    y_sample = jnp.transpose(_rownorm(xs, g_final, F32).reshape(steps, nbs, D_MODEL), (1, 0, 2))
    ssd_s = new_states.reshape(DEPTH, nbs, SSD_HEADS, SSD_HEADDIM, SSD_STATE)
    return (y_prompt, y_sample, jnp.stack(outs["ssd_p"]), jnp.stack(outs["conv_p"]), jnp.stack(outs["v_p"]),
            ssd_s, jnp.stack(outs["conv_s"]), jnp.stack(outs["v_s"]))
```

```python
import functools
from typing import Any, NamedTuple

import jax
import jax.numpy as jnp
from jax import lax
from jax.experimental import pallas as pl
from jax.experimental.pallas import tpu as pltpu

F32 = jnp.float32
BF16 = jnp.bfloat16

D_MODEL = 2048
DEPTH = 4
PLE_DIM = 256
RMS_EPS = 1e-6
SSD_INNER = 4096
SSD_HEADDIM = 64
SSD_HEADS = 64
SSD_GROUPS = 8
SSD_HPG = 8
SSD_STATE = 128
SSD_CHUNK = 128
CONV_W = 4
CONV_DIM = 6144
BC_COLS = 2 * SSD_GROUPS * SSD_STATE
CMLP_WIDTH = 2048
CMLP_GROUPS = 16
CMLP_GROUP_DIM = 128
CMLP_CHUNK = 128
FFN_HIDDEN = 5632
OFF_Z = SSD_INNER
OFF_XBC = OFF_Z + CONV_DIM
OFF_DT = OFF_XBC + SSD_HEADS
IN_WIDTH = OFF_DT + 4 * D_MODEL
GROUP_COLS = SSD_HPG * SSD_HEADDIM
LANES = 128
SUBLANES = 8
HALF_LANES = LANES // 2
VMEM_LIMIT_MB = 58
WIDE_TILE_VMEM_MB = 62
SMALL_GROUP_TN = 256
NT_DIMS = (((1,), (1,)), ((), ()))


def _params(n_axes, vmem_mb=VMEM_LIMIT_MB):
    return pltpu.CompilerParams(dimension_semantics=("arbitrary",) * n_axes,
                                vmem_limit_bytes=vmem_mb * 2 ** 20)


def _softplus(x):
    return jnp.maximum(x, 0.0) + jnp.log1p(jnp.exp(-jnp.abs(x)))


def _split(q, parts):
    out, rem = [], q
    for _ in range(parts):
        piece = rem.astype(BF16)
        out.append(piece)
        rem = rem - piece.astype(F32)
    return jnp.concatenate(out, axis=1)


def _inv_rms(ss_parts, width):
    return lax.rsqrt(jnp.sum(ss_parts, axis=0) / width + RMS_EPS)


def _rownorm_body(x_ref, g_ref, o_ref):
    xf = x_ref[...]
    r = lax.rsqrt(jnp.mean(xf * xf, axis=-1, keepdims=True) + RMS_EPS)
    o_ref[...] = ((xf * r) * g_ref[...]).astype(o_ref.dtype)


def _rownorm(x, g, out_dtype, tr=512):
    m, d = x.shape
    return pl.pallas_call(
        _rownorm_body,
        grid=(m // tr,),
        in_specs=[pl.BlockSpec((tr, d), lambda i: (i, 0)),
                  pl.BlockSpec((1, d), lambda i: (0, 0))],
        out_specs=pl.BlockSpec((tr, d), lambda i: (i, 0)),
        out_shape=jax.ShapeDtypeStruct((m, d), out_dtype),
        compiler_params=_params(1, 32),
        name="rownorm",
    )(x, g.reshape(1, d))


def _prenorm_body(x_ref, g_ref, xg_ref, ss_ref):
    xf = x_ref[...]
    xg_ref[...] = (xf * g_ref[...]).astype(xg_ref.dtype)
    ss_ref[...] = jnp.sum(xf * xf, axis=-1, keepdims=True)


def _prenorm(x, g, tr=512):
    m, d = x.shape
    return pl.pallas_call(
        _prenorm_body,
        grid=(m // tr,),
        in_specs=[pl.BlockSpec((tr, d), lambda i: (i, 0)),
                  pl.BlockSpec((1, d), lambda i: (0, 0))],
        out_specs=[pl.BlockSpec((tr, d), lambda i: (i, 0)),
                   pl.BlockSpec((None, tr, 1), lambda i: (0, i, 0))],
        out_shape=[jax.ShapeDtypeStruct((m, d), BF16), jax.ShapeDtypeStruct((1, m, 1), F32)],
        compiler_params=_params(1, 32),
        name="prenorm",
    )(x, g.reshape(1, d))


class Dot(NamedTuple):
    act: int
    w: Any
    off: int = 0
    scaled: bool = False
    transposed: bool = False
    shifted: bool = False


def _fused_mm_body(*refs, n_acts, dots, n_extras, epi, has_ss, has_gain):
    a_refs = refs[:n_acts]
    pos = n_acts
    w_refs = []
    for d in dots:
        w_refs.append(refs[pos:pos + (2 if d.shifted else 1)])
        pos += 2 if d.shifted else 1
    ss_ref = refs[pos] if has_ss else None
    pos += has_ss
    e_refs = refs[pos:pos + n_extras]
    pos += n_extras
    gain_ref = refs[pos] if has_gain else None
    pos += has_gain
    o_ref = refs[pos]
    pos += 1
    if has_gain:
        xg_ref, ssq_ref = refs[pos:pos + 2]
        pos += 2
    scr = refs[pos:]

    @pl.when(pl.program_id(1) == 0)
    def _():
        for d, wr, s in zip(dots, w_refs, scr):
            if d.shifted:
                keep = s.shape[0] - HALF_LANES
                s[:keep, :] = wr[0][HALF_LANES:, :].astype(BF16)
                s[keep:, :] = wr[1][...].astype(BF16)
            else:
                s[...] = wr[0][...].astype(BF16)

    acts = [a[...].astype(BF16) for a in a_refs]
    r = _inv_rms(ss_ref[...], D_MODEL) if has_ss else None
    accs = []
    for d, s in zip(dots, scr):
        if d.transposed:
            acc = lax.dot_general(acts[d.act], s[...], NT_DIMS, preferred_element_type=F32)
        else:
            acc = jnp.dot(acts[d.act], s[...], preferred_element_type=F32)
        accs.append(acc * r if d.scaled else acc)
    out = epi(accs, [e[...] for e in e_refs])
    o_ref[...] = out.astype(o_ref.dtype)
    if has_gain:
        xg_ref[...] = (out * gain_ref[...]).astype(xg_ref.dtype)
        ssq_ref[...] = jnp.sum(out * out, axis=-1, keepdims=True)


def _fused_mm(acts, dots, extras, epi, *, layer, n, tm, tn, out_dtype, name, ss=None, next_gain=None,
              vmem_mb=VMEM_LIMIT_MB):
    m = acts[0].shape[-2]
    tm = min(tm, m)
    in_specs, args, scratch = [], [], []
    for a in acts:
        if a.ndim == 3:
            in_specs.append(pl.BlockSpec((None, tm, a.shape[2]), lambda j, i: (layer, i, 0)))
        else:
            in_specs.append(pl.BlockSpec((tm, a.shape[1]), lambda j, i: (i, 0)))
        args.append(a)
    for d in dots:
        k = acts[d.act].shape[-1]
        if d.transposed:
            in_specs.append(pl.BlockSpec((None, tn, k), lambda j, i, off=d.off: (layer, j + off, 0)))
            args.append(d.w)
            if d.shifted:
                per = tn // HALF_LANES
                in_specs.append(pl.BlockSpec((None, HALF_LANES, k),
                                             lambda j, i, off=d.off, per=per: (layer, (j + off + 1) * per, 0)))
                args.append(d.w)
            scratch.append(pltpu.VMEM((tn, k), BF16))
        else:
            in_specs.append(pl.BlockSpec((None, k, tn), lambda j, i, off=d.off: (layer, 0, j + off)))
            args.append(d.w)
            scratch.append(pltpu.VMEM((k, tn), BF16))
    if ss is not None:
        in_specs.append(pl.BlockSpec((ss.shape[0], tm, 1), lambda j, i: (0, i, 0)))
        args.append(ss)
    for e, off in extras:
        in_specs.append(pl.BlockSpec((tm, tn), lambda j, i, off=off: (i, j + off)))
        args.append(e)
    out_specs = [pl.BlockSpec((tm, tn), lambda j, i: (i, j))]
    out_shape = [jax.ShapeDtypeStruct((m, n), out_dtype)]
    if next_gain is not None:
        in_specs.append(pl.BlockSpec((1, tn), lambda j, i: (0, j)))
        args.append(next_gain.reshape(1, n))
        out_specs += [pl.BlockSpec((tm, tn), lambda j, i: (i, j)),
                      pl.BlockSpec((None, tm, 1), lambda j, i: (j, i, 0))]
        out_shape += [jax.ShapeDtypeStruct((m, n), BF16), jax.ShapeDtypeStruct((n // tn, m, 1), F32)]
    body = functools.partial(_fused_mm_body, n_acts=len(acts), dots=tuple(d._replace(w=None) for d in dots),
                             n_extras=len(extras), epi=epi, has_ss=ss is not None,
                             has_gain=next_gain is not None)
    res = pl.pallas_call(
        body,
        grid=(n // tn, m // tm),
        in_specs=in_specs,
        out_specs=out_specs,
        out_shape=out_shape,
        scratch_shapes=scratch,
        compiler_params=_params(2, vmem_mb),
        name=name,
    )(*args)
    return res if next_gain is not None else res[0]


def _epi_first(accs, extras):
    return accs[0]


def _epi_residual(accs, extras):
    return extras[0] + accs[0]


def _epi_gate_merge(accs, extras):
    return (jax.nn.sigmoid(extras[0].astype(F32)) * accs[0]
            + jax.nn.sigmoid(extras[1].astype(F32)) * accs[1])


def _epi_swiglu(accs, extras):
    return jax.nn.silu(accs[0]) * accs[1]


def _epi_ple(accs, extras):
    return extras[0] + jax.nn.sigmoid(accs[0]) * accs[1]


def _conv_silu_rows(raw_ref, prev_rows, w_ref, b_ref):
    n_rows = raw_ref.shape[0]
    head = raw_ref[:SUBLANES, :]
    ext = jnp.concatenate([prev_rows, head], axis=0)
    acc_head = b_ref[...] + w_ref[CONV_W - 1:CONV_W, :] * head
    acc_rest = b_ref[...] + w_ref[CONV_W - 1:CONV_W, :] * raw_ref[SUBLANES:, :]
    for j in range(1, CONV_W):
        wj = w_ref[CONV_W - 1 - j:CONV_W - j, :]
        acc_head = acc_head + wj * pltpu.roll(ext, j, axis=0)[SUBLANES:, :]
        acc_rest = acc_rest + wj * raw_ref[pl.ds(SUBLANES - j, n_rows - SUBLANES), :]
    acc = jnp.concatenate([acc_head, acc_rest], axis=0)
    return acc * jax.nn.sigmoid(acc)


def _conv_sample_body(x_ref, s_ref, w_ref, b_ref, o_ref):
    steps = x_ref.shape[0]
    xpad = [s_ref[k] for k in range(CONV_W - 1)] + [x_ref[t] for t in range(steps)]
    for t in range(steps):
        acc = b_ref[...] + w_ref[0:1, :] * xpad[t]
        for k in range(1, CONV_W):
            acc = acc + w_ref[k:k + 1, :] * xpad[t + k]
        o_ref[t] = acc * jax.nn.sigmoid(acc)


def _conv_sample(xbc_t, state_t, conv_w, conv_b, layer, tc=1024):
    steps, nb, _ = xbc_t.shape
    return pl.pallas_call(
        _conv_sample_body,
        grid=(CONV_DIM // tc,),
        in_specs=[pl.BlockSpec((steps, nb, tc), lambda c: (0, 0, c)),
                  pl.BlockSpec((None, CONV_W - 1, nb, tc), lambda c: (layer, 0, 0, c)),
                  pl.BlockSpec((None, CONV_W, tc), lambda c: (layer, 0, c)),
                  pl.BlockSpec((1, tc), lambda c: (0, c))],
        out_specs=pl.BlockSpec((steps, nb, tc), lambda c: (0, 0, c)),
        out_shape=jax.ShapeDtypeStruct((steps, nb, CONV_DIM), F32),
        compiler_params=_params(1, 32),
        name="conv_sample",
    )(xbc_t, state_t, conv_w, conv_b.reshape(1, CONV_DIM))


HEADS_PER_DOT = 4
QUAD_COLS = HEADS_PER_DOT * SSD_HEADDIM
CHUNKS_PER_STEP = 2


def _gated_norm(y2, g):
    r = lax.rsqrt(jnp.mean(y2 * y2, axis=-1, keepdims=True) + RMS_EPS)
    return (y2 * r) * g


def _ssd_prompt_body(z_ref, xr_ref, bcr_ref, dtp_ref, cwx_ref, cwb_ref, cbx_ref, cbb_ref,
                     bias_ref, alog_ref, dsk_ref, g_ref, e2_ref,
                     yn_ref, st_ref, s_scr, y_scr, prev_x, prev_bc, *, n_steps):
    c = pl.program_id(1)
    q = SSD_CHUNK
    rows_n = q * CHUNKS_PER_STEP

    @pl.when(c == 0)
    def _():
        s_scr[...] = jnp.zeros_like(s_scr)
        prev_x[...] = jnp.zeros_like(prev_x)
        prev_bc[...] = jnp.zeros_like(prev_bc)

    xs_all = _conv_silu_rows(xr_ref, prev_x[...], cwx_ref, cbx_ref)
    bc_all = _conv_silu_rows(bcr_ref, prev_bc[...], cwb_ref, cbb_ref)
    prev_x[...] = xr_ref[rows_n - SUBLANES:, :]
    prev_bc[...] = bcr_ref[rows_n - SUBLANES:, :]

    lane = lax.broadcasted_iota(jnp.int32, (q, LANES), 1)
    row = lax.broadcasted_iota(jnp.int32, (q, LANES), 0)
    head_ok = lane < SSD_HEADS
    causal = row >= lane
    tri = causal.astype(BF16)
    lane_q = lax.broadcasted_iota(jnp.int32, (q, QUAD_COLS), 1) // SSD_HEADDIM
    head_masks = [(lane_q == k).astype(BF16) for k in range(HEADS_PER_DOT)]
    a = -jnp.exp(alog_ref[...])

    for sub in range(CHUNKS_PER_STEP):
        rs = slice(sub * q, (sub + 1) * q)
        dt = jnp.where(head_ok, _softplus(dtp_ref[rs, :] + bias_ref[...]), 0.0)
        dta = dt * a
        cum3 = jnp.dot(tri, _split(dta, 3), preferred_element_type=F32)
        cum = cum3[:, :LANES] + cum3[:, LANES:2 * LANES] + cum3[:, 2 * LANES:]
        cum_t = (cum - jnp.where(head_ok, jnp.log(dt), 0.0)).T
        last = cum[q - 1:q, :]
        ecum = jnp.exp(cum)
        dtw = jnp.where(head_ok, dt * jnp.exp(last - cum), 0.0)
        stack = jnp.concatenate([dtw, ecum], axis=0)
        wide = jnp.dot(_split(stack, 2), e2_ref[...], preferred_element_type=F32)
        dtw_w, ecum_w = wide[:q], wide[q:]
        cdec_w = ecum_w[q - 1:q, :]

        xs = xs_all[rs]
        xs_b = xs.astype(BF16)
        xdd_b = (xs * dtw_w).astype(BF16)
        for g in range(SSD_GROUPS):
            gs = slice(g * GROUP_COLS, (g + 1) * GROUP_COLS)
            b_g = bc_all[rs, g * SSD_STATE:(g + 1) * SSD_STATE]
            c_g = bc_all[rs, (SSD_GROUPS + g) * SSD_STATE:(SSD_GROUPS + g + 1) * SSD_STATE]
            c_b = c_g.astype(BF16)
            cb = lax.dot_general(c_b, b_g.astype(BF16), NT_DIMS, preferred_element_type=F32)
            s_g = s_scr[g]
            y_off = jnp.dot(c_b, s_g.astype(BF16), preferred_element_type=F32)
            y_quads = []
            for quad in range(SSD_HPG // HEADS_PER_DOT):
                h0 = g * SSD_HPG + quad * HEADS_PER_DOT
                m_heads = []
                for k in range(HEADS_PER_DOT):
                    h = h0 + k
                    diff = cum[:, h:h + 1] - cum_t[h:h + 1, :]
                    seg = jnp.exp(jnp.where(causal, diff, -jnp.inf))
                    m_heads.append((cb * seg).astype(BF16))
                xq = xs_b[:, h0 * SSD_HEADDIM:h0 * SSD_HEADDIM + QUAD_COLS]
                rhs = jnp.concatenate([xq * head_masks[k] for k in range(HEADS_PER_DOT)], axis=0)
                y_quads.append(jnp.dot(jnp.concatenate(m_heads, axis=1), rhs, preferred_element_type=F32))
            y_g = jnp.concatenate(y_quads, axis=1) + y_off * ecum_w[:, gs]
            upd = jnp.dot(b_g.T.astype(BF16), xdd_b[:, gs], preferred_element_type=F32)
            s_new = s_g * cdec_w[:, gs] + upd
            s_scr[g] = s_new

            if sub == CHUNKS_PER_STEP - 1:
                @pl.when(c == n_steps - 1)
                def _(s_new=s_new, gs=gs):
                    st_ref[0, gs, :] = s_new.T

            zg = z_ref[rs, gs].astype(F32)
            y_scr[rs, gs] = (y_g + xs[:, gs] * dsk_ref[:, gs]) * (zg * jax.nn.sigmoid(zg))
    yn_ref[...] = _gated_norm(y_scr[...], g_ref[...]).astype(yn_ref.dtype)


def _ssd_prompt(z, xbc, dtp, conv_w, conv_b, layer, bias, alog, dsk_w, g_ssd, e2, nb, seq):
    rows_n = SSD_CHUNK * CHUNKS_PER_STEP
    ns = seq // rows_n
    body = functools.partial(_ssd_prompt_body, n_steps=ns)
    const = lambda b, c: (0, 0)
    rows = lambda b, c: (b * ns + c, 0)
    cb2 = conv_b.reshape(1, CONV_DIM)
    return pl.pallas_call(
        body,
        grid=(nb, ns),
        in_specs=[pl.BlockSpec((rows_n, SSD_INNER), rows),
                  pl.BlockSpec((rows_n, SSD_INNER), rows),
                  pl.BlockSpec((rows_n, BC_COLS), lambda b, c: (b * ns + c, SSD_INNER // BC_COLS)),
                  pl.BlockSpec((rows_n, LANES), rows),
                  pl.BlockSpec((None, CONV_W, SSD_INNER), lambda b, c: (layer, 0, 0)),
                  pl.BlockSpec((None, CONV_W, BC_COLS), lambda b, c: (layer, 0, SSD_INNER // BC_COLS)),
                  pl.BlockSpec((1, SSD_INNER), const),
                  pl.BlockSpec((1, BC_COLS), lambda b, c: (0, SSD_INNER // BC_COLS)),
                  pl.BlockSpec((1, LANES), const),
                  pl.BlockSpec((1, LANES), const),
                  pl.BlockSpec((1, SSD_INNER), const),
                  pl.BlockSpec((1, SSD_INNER), const),
                  pl.BlockSpec((2 * LANES, SSD_INNER), const)],
        out_specs=[pl.BlockSpec((rows_n, SSD_INNER), rows),
                   pl.BlockSpec((1, SSD_INNER, SSD_STATE), lambda b, c: (b, 0, 0))],
        out_shape=[jax.ShapeDtypeStruct((nb * seq, SSD_INNER), BF16),
                   jax.ShapeDtypeStruct((nb, SSD_INNER, SSD_STATE), F32)],
        scratch_shapes=[pltpu.VMEM((SSD_GROUPS, SSD_STATE, GROUP_COLS), F32),
                        pltpu.VMEM((rows_n, SSD_INNER), F32),
                        pltpu.VMEM((SUBLANES, SSD_INNER), F32),
                        pltpu.VMEM((SUBLANES, BC_COLS), F32)],
        compiler_params=_params(2, 56),
        name="ssd_prompt",
    )(z, xbc, xbc, dtp, conv_w, conv_w, cb2, cb2, bias, alog, dsk_w, g_ssd, e2)


SEQ_PER_STEP = 4


def _ssd_sample_body(xc_ref, z_ref, dtp_ref, h0_ref, bias_ref, alog_ref, dsk_ref, g_ref, e3_ref, *rest, steps):
    yn_ref, hn_ref = rest[-2:]
    rows_n = SEQ_PER_STEP * steps
    lane = lax.broadcasted_iota(jnp.int32, (rows_n, LANES), 1)
    row = lax.broadcasted_iota(jnp.int32, (rows_n, LANES), 0)
    tpos = row % steps
    head_ok = lane < SSD_HEADS

    def shift(v, j, tp):
        return jnp.where(tp >= j, pltpu.roll(v, j, axis=0), 0.0)

    dt = jnp.where(head_ok, _softplus(dtp_ref[...] + bias_ref[...]), 0.0)
    a = -jnp.exp(alog_ref[...])
    dta = dt * a
    cum = dta
    for j in range(1, steps):
        cum = cum + shift(dta, j, tpos)
    last = jnp.zeros_like(cum)
    for k in range(steps):
        back = steps - 1 - k
        src = cum if back == 0 else pltpu.roll(cum, rows_n - back, axis=0)
        last = jnp.where(tpos == k, src, last)
    ecum = jnp.exp(cum)
    dte = jnp.where(head_ok, jnp.exp(last - cum), 0.0)
    cdec = jnp.exp(last)

    b_all = xc_ref[:, SSD_INNER:SSD_INNER + SSD_GROUPS * SSD_STATE]
    c_all = xc_ref[:, SSD_INNER + SSD_GROUPS * SSD_STATE:]
    row_w = lax.broadcasted_iota(jnp.int32, b_all.shape, 0)
    tpos_w = row_w % steps
    coefs = []
    for j in range(steps):
        prod = c_all * (b_all if j == 0 else shift(b_all, j, tpos_w))
        cb = jnp.zeros((rows_n, LANES), F32)
        for g in range(SSD_GROUPS):
            cbg = jnp.sum(prod[:, g * SSD_STATE:(g + 1) * SSD_STATE], axis=1, keepdims=True)
            cb = jnp.where((lane // SSD_HPG) == g, cbg, cb)
        if j == 0:
            coefs.append(jnp.where(head_ok, cb, 0.0))
        else:
            seg = jnp.exp(cum - shift(cum, j, tpos))
            coefs.append(jnp.where(head_ok & (tpos >= j), cb * seg, 0.0))
    stack = jnp.concatenate([dt, dte, ecum, cdec] + coefs, axis=0)
    wide = jnp.dot(_split(stack, 3), e3_ref[...], preferred_element_type=F32)
    dt_w, dte_w = wide[:rows_n], wide[rows_n:2 * rows_n]
    ecum_w, cdec_w = wide[2 * rows_n:3 * rows_n], wide[3 * rows_n:4 * rows_n]

    xs = xc_ref[:, :SSD_INNER]
    xdt = xs * dt_w
    y = wide[4 * rows_n:5 * rows_n] * xdt
    for j in range(1, steps):
        y = y + wide[(4 + j) * rows_n:(5 + j) * rows_n] * pltpu.roll(xdt, j, axis=0)
    xdd = xdt * dte_w

    row_g = lax.broadcasted_iota(jnp.int32, (rows_n, GROUP_COLS), 0)
    ones_b = jnp.ones((rows_n, SSD_STATE), BF16)
    c_bf = c_all.astype(BF16)
    b_bf = b_all.astype(BF16)
    y_off_groups = []
    for g in range(SSD_GROUPS):
        gs = slice(g * GROUP_COLS, (g + 1) * GROUP_COLS)
        ns = slice(g * SSD_STATE, (g + 1) * SSD_STATE)
        y_off = jnp.zeros((rows_n, GROUP_COLS), F32)
        for i in range(SEQ_PER_STEP):
            mine = (row_g // steps) == i
            h0g = h0_ref[i, gs, :]
            yo = lax.dot_general(c_bf[:, ns], h0g.astype(BF16), NT_DIMS, preferred_element_type=F32)
            y_off = jnp.where(mine, yo, y_off)
            x_i = jnp.where(mine, xdd[:, gs], 0.0).astype(BF16)
            upd = lax.dot_general(x_i, b_bf[:, ns], (((0,), (0,)), ((), ())),
                                  preferred_element_type=F32)
            cd = cdec_w[i * steps:i * steps + 1, gs]
            cd_hi = cd.astype(BF16).astype(F32)
            cd_mid = (cd - cd_hi).astype(BF16).astype(F32)
            cd_lo = cd - cd_hi - cd_mid
            cd_rows = jnp.where(row_g == 0, cd_hi,
                                jnp.where(row_g == 1, cd_mid,
                                          jnp.where(row_g == 2, cd_lo, 0.0))).astype(BF16)
            cd_col = lax.dot_general(cd_rows, ones_b, (((0,), (0,)), ((), ())),
                                     preferred_element_type=F32)
            hn_ref[i, gs, :] = h0g * cd_col + upd
        y_off_groups.append(y_off)
    y = y + jnp.concatenate(y_off_groups, axis=1) * ecum_w
    zz = z_ref[...].astype(F32)
    y2 = (y + xs * dsk_ref[...]) * (zz * jax.nn.sigmoid(zz))
    yn_ref[...] = _gated_norm(y2, g_ref[...]).astype(yn_ref.dtype)


def _ssd_sample(xc, z, dtp, h_all, layer, new_states, bias, alog, dsk_w, g_ssd, e3, steps):
    nb = h_all.shape[1]
    rows_n = SEQ_PER_STEP * steps
    body = functools.partial(_ssd_sample_body, steps=steps)
    const = lambda i: (0, 0)
    state_spec = pl.BlockSpec((None, SEQ_PER_STEP, SSD_INNER, SSD_STATE), lambda i: (layer, i, 0, 0))
    in_specs = [pl.BlockSpec((rows_n, CONV_DIM), lambda i: (i, 0)),
                pl.BlockSpec((rows_n, SSD_INNER), lambda i: (i, 0)),
                pl.BlockSpec((rows_n, LANES), lambda i: (i, 0)),
                state_spec,
                pl.BlockSpec((1, LANES), const),
                pl.BlockSpec((1, LANES), const),
                pl.BlockSpec((1, SSD_INNER), const),
                pl.BlockSpec((1, SSD_INNER), const),
                pl.BlockSpec((3 * LANES, SSD_INNER), const)]
    args = [xc, z, dtp, h_all, bias, alog, dsk_w, g_ssd, e3]
    aliases = {}
    if new_states is not None:
        in_specs.append(pl.BlockSpec(memory_space=pl.ANY))
        args.append(new_states)
        aliases = {len(args) - 1: 1}
    return pl.pallas_call(
        body,
        grid=(nb // SEQ_PER_STEP,),
        in_specs=in_specs,
        out_specs=[pl.BlockSpec((rows_n, SSD_INNER), lambda i: (i, 0)), state_spec],
        out_shape=[jax.ShapeDtypeStruct((nb * steps, SSD_INNER), BF16),
                   jax.ShapeDtypeStruct(h_all.shape, F32)],
        input_output_aliases=aliases,
        compiler_params=_params(1, 56),
        name="ssd_sample",
    )(*args)


def _vnorm(v, g):
    v = jax.nn.gelu(v)
    r = lax.rsqrt(jnp.mean(v * v, axis=-1, keepdims=True) + RMS_EPS)
    return (v * r) * g


CMLP_CHUNKS_PER_STEP = 4


def _cmlp_prompt_body(u_ref, v_ref, ws_ref, bst_ref, gv_ref, cm_ref, vs_ref, *, n_steps):
    q = CMLP_CHUNK
    row = lax.broadcasted_iota(jnp.int32, (q, q), 0)
    col = lax.broadcasted_iota(jnp.int32, (q, q), 1)
    wms = [jnp.where(row >= col, ws_ref[g], 0.0).astype(BF16) for g in range(CMLP_GROUPS)]
    for sub in range(CMLP_CHUNKS_PER_STEP):
        rs = slice(sub * q, (sub + 1) * q)
        vn = _vnorm(v_ref[rs, :].astype(F32), gv_ref[...])
        if sub == CMLP_CHUNKS_PER_STEP - 1:
            @pl.when(pl.program_id(0) % n_steps == n_steps - 1)
            def _(vn=vn):
                vs_ref[0] = vn

        u = jax.nn.gelu(u_ref[rs, :].astype(F32))
        vb = vn.astype(BF16)
        for g in range(CMLP_GROUPS):
            gs = slice(g * CMLP_GROUP_DIM, (g + 1) * CMLP_GROUP_DIM)
            mixed = jnp.dot(wms[g], vb[:, gs], preferred_element_type=F32) + bst_ref[:, g:g + 1]
            cm_ref[rs, gs] = (u[:, gs] * mixed).astype(cm_ref.dtype)


def _cmlp_prompt(uvg, w_s, b_s, g_v, layer, nb, seq):
    q = CMLP_CHUNK
    rows_n = q * CMLP_CHUNKS_PER_STEP
    nc = seq // rows_n
    body = functools.partial(_cmlp_prompt_body, n_steps=nc)
    return pl.pallas_call(
        body,
        grid=(nb * nc,),
        in_specs=[pl.BlockSpec((rows_n, CMLP_WIDTH), lambda i: (i, 0)),
                  pl.BlockSpec((rows_n, CMLP_WIDTH), lambda i: (i, 1)),
                  pl.BlockSpec((None, CMLP_GROUPS, q, q), lambda i: (layer, 0, 0, 0)),
                  pl.BlockSpec((q, CMLP_GROUPS), lambda i: (0, 0)),
                  pl.BlockSpec((1, CMLP_WIDTH), lambda i: (0, 0))],
        out_specs=[pl.BlockSpec((rows_n, CMLP_WIDTH), lambda i: (i, 0)),
                   pl.BlockSpec((1, q, CMLP_WIDTH), lambda i: (i // nc, 0, 0))],
        out_shape=[jax.ShapeDtypeStruct((nb * seq, CMLP_WIDTH), BF16),
                   jax.ShapeDtypeStruct((nb, q, CMLP_WIDTH), F32)],
        compiler_params=_params(1, 32),
        name="cmlp_prompt",
    )(uvg, uvg, w_s, b_s.T, g_v.reshape(1, CMLP_WIDTH))


def _cmlp_sample_body(u_ref, v_ref, wrow_ref, brow_ref, gv_ref, cm_ref, vs_ref):
    steps = u_ref.shape[0]
    vn = [_vnorm(v_ref[t].astype(F32), gv_ref[...]) for t in range(steps)]
    for t in range(steps):
        vs_ref[t] = vn[t]
        mixed = brow_ref[t:t + 1, :] + wrow_ref[t, 0:1, :] * vn[0]
        for s in range(1, t + 1):
            mixed = mixed + wrow_ref[t, s:s + 1, :] * vn[s]
        cm_ref[t] = (jax.nn.gelu(u_ref[t].astype(F32)) * mixed).astype(cm_ref.dtype)


def _cmlp_sample(uvg_t, w_s, b_s, g_v, tb=32):
    steps, nb, _ = uvg_t.shape
    wrow = jnp.repeat(jnp.transpose(w_s[:, :steps, :steps], (1, 2, 0)), CMLP_GROUP_DIM, axis=-1)
    brow = jnp.repeat(b_s[:, :steps].T, CMLP_GROUP_DIM, axis=-1)
    return pl.pallas_call(
        _cmlp_sample_body,
        grid=(nb // tb,),
        in_specs=[pl.BlockSpec((steps, tb, CMLP_WIDTH), lambda i: (0, i, 0)),
                  pl.BlockSpec((steps, tb, CMLP_WIDTH), lambda i: (0, i, 1)),
                  pl.BlockSpec((steps, steps, CMLP_WIDTH), lambda i: (0, 0, 0)),
                  pl.BlockSpec((steps, CMLP_WIDTH), lambda i: (0, 0)),
                  pl.BlockSpec((1, CMLP_WIDTH), lambda i: (0, 0))],
        out_specs=[pl.BlockSpec((steps, tb, CMLP_WIDTH), lambda i: (0, i, 0)),
                   pl.BlockSpec((steps, tb, CMLP_WIDTH), lambda i: (0, i, 0))],
        out_shape=[jax.ShapeDtypeStruct((steps, nb, CMLP_WIDTH), BF16),
                   jax.ShapeDtypeStruct((steps, nb, CMLP_WIDTH), F32)],
        compiler_params=_params(1, 32),
        name="cmlp_sample",
    )(uvg_t, uvg_t, wrow, brow, g_v.reshape(1, CMLP_WIDTH))


def _layer(x, hg, ss, ple_all, w, gains, layer, mixers, big):
    tm = 1024 if big else 512
    mm = functools.partial(_fused_mm, layer=layer)
    w_in_t = w["w_in_t"]
    t1, t2 = (1024, 512) if big else (SMALL_GROUP_TN, SMALL_GROUP_TN)
    z = mm([hg], [Dot(0, w_in_t, 0, scaled=True, transposed=True)], [], _epi_first, ss=ss,
           n=OFF_Z, tm=2 * tm, tn=t1, out_dtype=BF16, vmem_mb=WIDE_TILE_VMEM_MB, name="in_proj_z")
    xbc = mm([hg], [Dot(0, w_in_t, OFF_Z // t1, scaled=True, transposed=True)], [], _epi_first, ss=ss,
             n=CONV_DIM, tm=tm, tn=t1, out_dtype=F32, name="in_proj_xbc")
    dtp = mm([hg], [Dot(0, w_in_t, OFF_XBC // LANES, scaled=True, transposed=True)], [], _epi_first, ss=ss,
             n=LANES, tm=tm, tn=LANES, out_dtype=F32, name="in_proj_dt")
    uvg = mm([hg], [Dot(0, w_in_t, OFF_XBC // t1, scaled=True, transposed=True, shifted=True)], [],
             _epi_first, ss=ss, n=4 * D_MODEL, tm=2 * tm, tn=t1, out_dtype=BF16, vmem_mb=WIDE_TILE_VMEM_MB,
             name="in_proj_uvg")
    yn, cm, states = mixers(z, xbc, dtp, uvg)
    merged = mm([yn, cm], [Dot(0, w["w_br_a"]), Dot(1, w["w_br_b"])],
                [(uvg, 2 * D_MODEL // t2), (uvg, 3 * D_MODEL // t2)], _epi_gate_merge,
                n=D_MODEL, tm=512, tn=t2, out_dtype=BF16, name="branch_merge")
    x, hn, ss = mm([merged], [Dot(0, w["w_out"])], [(x, 0)], _epi_residual, next_gain=gains["g_ffn"],
                   n=D_MODEL, tm=tm, tn=t1, out_dtype=F32, name="out_proj")
    act = mm([hn], [Dot(0, w["w_gate_up"], 0, scaled=True), Dot(0, w["w_gate_up"], FFN_HIDDEN // t2, scaled=True)],
             [], _epi_swiglu, ss=ss, n=FFN_HIDDEN, tm=2 * tm, tn=t2, out_dtype=BF16, name="ffn_up")
    x, hp, ss = mm([act], [Dot(0, w["w_down"])], [(x, 0)], _epi_residual, next_gain=gains["g_ple"],
                   n=D_MODEL, tm=512, tn=t2, out_dtype=F32, name="ffn_down")
    ple_dots = [Dot(0, w["w_ple_gate"], 0, scaled=True), Dot(1, w["w_ple"])]
    if gains["g_next"] is None:
        x = mm([hp, ple_all], ple_dots, [(x, 0)], _epi_ple, ss=ss,
               n=D_MODEL, tm=512, tn=t1, out_dtype=F32, name="ple")
        return x, None, None, states
    x, hg, ss = mm([hp, ple_all], ple_dots, [(x, 0)], _epi_ple, ss=ss, next_gain=gains["g_next"],
                   n=D_MODEL, tm=512, tn=t1, out_dtype=F32, name="ple")
    return x, hg, ss, states


def kernel(x_prompt, x_sample, p_prompt, p_sample, state_ssd, state_conv, g_mix, w_in, conv_w, conv_b,
           dt_bias, a_log, d_skip, g_ssd, w_br_a, g_v, w_s, b_s, w_br_b, w_out, g_ffn, w_gate_up, w_down,
           g_ple, w_ple_gate, w_ple, g_final):
    nbp, seq, _ = x_prompt.shape
    nbs, steps, _ = x_sample.shape
    assert seq % (SSD_CHUNK * CHUNKS_PER_STEP) == 0 and seq % (CMLP_CHUNK * CMLP_CHUNKS_PER_STEP) == 0
    assert steps <= SSD_CHUNK

    head_rows = lax.broadcasted_iota(jnp.int32, (LANES, SSD_INNER), 0)
    head_cols = lax.broadcasted_iota(jnp.int32, (LANES, SSD_INNER), 1) // SSD_HEADDIM
    expand = (head_rows == head_cols).astype(BF16)
    e2, e3 = jnp.tile(expand, (2, 1)), jnp.tile(expand, (3, 1))
    pad_heads = lambda v: jnp.pad(v, (0, LANES - SSD_HEADS)).reshape(1, LANES)
    w_in_t = jnp.transpose(w_in, (0, 2, 1))
    w = dict(w_in_t=w_in_t, w_br_a=w_br_a, w_br_b=w_br_b, w_out=w_out,
             w_gate_up=w_gate_up, w_down=w_down, w_ple_gate=w_ple_gate, w_ple=w_ple)

    xp = x_prompt.reshape(nbp * seq, D_MODEL)
    xs = jnp.transpose(x_sample, (1, 0, 2)).reshape(steps * nbs, D_MODEL)
    ple_p = p_prompt.reshape(DEPTH, nbp * seq, PLE_DIM)
    ple_s = jnp.transpose(p_sample, (0, 2, 1, 3)).reshape(DEPTH, steps * nbs, PLE_DIM)
    conv_state_t = jnp.transpose(state_conv, (0, 2, 1, 3))
    h_all = state_ssd.reshape(DEPTH, nbs, SSD_INNER, SSD_STATE)
    new_states = None
    hgp, ssp = _prenorm(xp, g_mix[0])
    hgs, sss = _prenorm(xs, g_mix[0])
    outs = {k: [] for k in ("ssd_p", "conv_p", "v_p", "conv_s", "v_s")}
    for i in range(DEPTH):
        gains = dict(g_ffn=g_ffn[i], g_ple=g_ple[i], g_next=g_mix[i + 1] if i + 1 < DEPTH else None)
        bias, alog = pad_heads(dt_bias[i]), pad_heads(a_log[i])
        dsk_w = jnp.repeat(d_skip[i], SSD_HEADDIM).reshape(1, SSD_INNER)
        gs = g_ssd[i].reshape(1, SSD_INNER)

        def prompt_mixers(z, xbc, dtp, uvg, i=i, bias=bias, alog=alog, dsk_w=dsk_w, gs=gs):
            yn, st = _ssd_prompt(z, xbc, dtp, conv_w, conv_b[i], i, bias, alog, dsk_w, gs, e2, nbp, seq)
            cm, vst = _cmlp_prompt(uvg, w_s, b_s[i], g_v[i], i, nbp, seq)
            conv_new = xbc.reshape(nbp, seq, CONV_DIM)[:, seq - (CONV_W - 1):, :]
            return yn, cm, (st.reshape(nbp, SSD_HEADS, SSD_HEADDIM, SSD_STATE), conv_new, vst)

        def sample_mixers(z, xbc, dtp, uvg, i=i, bias=bias, alog=alog, dsk_w=dsk_w, gs=gs, new_states=new_states):
            xbc_t = xbc.reshape(steps, nbs, CONV_DIM)
            xc_t = _conv_sample(xbc_t, conv_state_t, conv_w, conv_b[i], i)
            seq_major = lambda t, wd: jnp.transpose(t.reshape(steps, nbs, wd), (1, 0, 2)).reshape(nbs * steps, wd)
            yn_b, st = _ssd_sample(seq_major(xc_t, CONV_DIM), seq_major(z, OFF_Z),
                                   seq_major(dtp, LANES), h_all, i, new_states,
                                   bias, alog, dsk_w, gs, e3, steps)
            yn = jnp.transpose(yn_b.reshape(nbs, steps, SSD_INNER), (1, 0, 2)).reshape(steps * nbs, SSD_INNER)
            cm_t, vn_t = _cmlp_sample(uvg.reshape(steps, nbs, 4 * D_MODEL), w_s[i], b_s[i], g_v[i])
            conv_new = jnp.transpose(xbc_t[steps - (CONV_W - 1):], (1, 0, 2))
            return (yn, cm_t.reshape(steps * nbs, CMLP_WIDTH),
                    (st, conv_new, jnp.transpose(vn_t, (1, 0, 2))))

        xp, hgp, ssp, (hp_, cp_, vp_) = _layer(xp, hgp, ssp, ple_p, w, gains, i, prompt_mixers, True)
        xs, hgs, sss, (new_states, cs_, vs_) = _layer(xs, hgs, sss, ple_s, w, gains, i, sample_mixers, False)
        outs["ssd_p"].append(hp_); outs["conv_p"].append(cp_); outs["v_p"].append(vp_)
        outs["conv_s"].append(cs_); outs["v_s"].append(vs_)

    y_prompt = _rownorm(xp, g_final, F32).reshape(nbp, seq, D_MODEL)
    y_sample = jnp.transpose(_rownorm(xs, g_final, F32).reshape(steps, nbs, D_MODEL), (1, 0, 2))
    ssd_s = new_states.reshape(DEPTH, nbs, SSD_HEADS, SSD_HEADDIM, SSD_STATE)
    return (y_prompt, y_sample, jnp.stack(outs["ssd_p"]), jnp.stack(outs["conv_p"]), jnp.stack(outs["v_p"]),
            ssd_s, jnp.stack(outs["conv_s"]), jnp.stack(outs["v_s"]))
```
